```python
import math
import jax, jax.numpy as jnp
from jax import lax
import numpy as np

D_MODEL = 1024
BATCH = 16
SEQ = 4096
DEPTH = 1
DEC_BATCH = 32
DEC_SEQ = 32
PAST_LEN = 1024

CHUNK = 64
D_CONV = 512
CONV_A_WIDTH = 3
N_HEADS_GDN = 4
HEAD_DIM = 128
D_GDN = N_HEADS_GDN * HEAD_DIM
D_QKV = 3 * D_GDN
CONV_QKV_WIDTH = 4
D_MIX = D_CONV + D_GDN
PROJ_SIZES = (D_CONV, D_CONV, D_CONV, D_CONV, D_QKV, D_GDN, N_HEADS_GDN, N_HEADS_GDN)
D_PROJ = 4 * D_CONV + D_QKV + D_GDN + 2 * N_HEADS_GDN
RMS_EPS = 1e-6
L2_EPS = 1e-6

kernel_name = "hybrid_shortconv_gated_deltanet_stream_step"


def rmsnorm(x, w):
    xf = x.astype(jnp.float32)
    var = jnp.mean(xf * xf, axis=-1, keepdims=True)
    return (xf * lax.rsqrt(var + RMS_EPS) * w.astype(jnp.float32)).astype(x.dtype)


def l2norm(x):
    xf = x.astype(jnp.float32)
    return xf * lax.rsqrt(jnp.sum(xf * xf, axis=-1, keepdims=True) + L2_EPS)


def causal_dwconv(u, buf, w):
    width = w.shape[0]
    t = u.shape[1]
    ucat = jnp.concatenate([buf.astype(u.dtype), u], axis=1)
    y = ucat[:, 0:t] * w[0]
    for j in range(1, width):
        y = y + ucat[:, j:j + t] * w[j]
    return y, ucat[:, t:]


def gated_delta_rule(q, k, v, g, beta, s0):
    f32 = jnp.float32
    bsz, t, h, dk = q.shape
    dv = v.shape[-1]
    c = math.gcd(t, CHUNK)
    n = t // c

    def blocks(a):
        a = a.astype(f32).reshape((bsz, n, c) + a.shape[2:])
        return jnp.moveaxis(a, 3, 1)

    qb = blocks(q) * (dk ** -0.5)
    kb = blocks(k)
    vb = blocks(v)
    gb = jnp.cumsum(blocks(g), axis=-1)
    bb = blocks(beta)

    incl = jnp.tril(jnp.ones((c, c), dtype=bool))
    strict = jnp.tril(jnp.ones((c, c), dtype=bool), -1)
    diff = gb[..., :, None] - gb[..., None, :]
    decay = jnp.where(incl, jnp.exp(jnp.where(incl, diff, 0.0)), 0.0)

    kbeta = kb * bb[..., None]
    a_mat = jnp.where(strict, jnp.einsum('bhnid,bhnjd->bhnij', kbeta, kb) * decay, 0.0)
    eye = jnp.broadcast_to(jnp.eye(c, dtype=f32), a_mat.shape)
    t_mat = lax.linalg.triangular_solve(a_mat, eye, left_side=True, lower=True,
                                        unit_diagonal=True)
    u = jnp.einsum('bhnij,bhnjd->bhnid', t_mat, vb * bb[..., None])
    w = jnp.einsum('bhnij,bhnjd->bhnid', t_mat, kbeta * jnp.exp(gb)[..., None])
    qg = qb * jnp.exp(gb)[..., None]
    attn = jnp.einsum('bhnid,bhnjd->bhnij', qb, kb) * decay
    g_last = gb[..., -1]
    kdec = kb * jnp.exp(g_last[..., None] - gb)[..., None]

    def step(s, xs):
        u_i, w_i, qg_i, attn_i, kdec_i, gl_i = xs
        v_new = u_i - jnp.einsum('bhcd,bhde->bhce', w_i, s)
        o_i = (jnp.einsum('bhcd,bhde->bhce', qg_i, s)
               + jnp.einsum('bhij,bhje->bhie', attn_i, v_new))
        s = s * jnp.exp(gl_i)[..., None, None] + jnp.einsum('bhcd,bhce->bhde', kdec_i, v_new)
        return s, o_i

    xs = (jnp.moveaxis(u, 2, 0), jnp.moveaxis(w, 2, 0), jnp.moveaxis(qg, 2, 0),
          jnp.moveaxis(attn, 2, 0), jnp.moveaxis(kdec, 2, 0), jnp.moveaxis(g_last, 2, 0))
    s_fin, o = lax.scan(step, s0.astype(f32), xs)
    o = jnp.moveaxis(o, 0, 2).reshape(bsz, h, t, dv).transpose(0, 2, 1, 3)
    return o, s_fin


def hybrid_layer(x, buf_a, buf_qkv, s0, ln_pre, w_in, conv_a_w, conv_qkv_w,
                 a_log, dt_bias, gdn_norm_w, w_out, ln_post):
    f32 = jnp.float32
    bsz, t, _ = x.shape
    h = rmsnorm(x, ln_pre)
    proj = jnp.einsum('btd,dp->btp', h, w_in)
    splits = np.cumsum(np.array(PROJ_SIZES))[:-1].tolist()
    b_a, c_a, x_a, z_a, qkv, z_g, a_dec, b_beta = jnp.split(proj, splits, axis=-1)

    conv_a, new_buf_a = causal_dwconv(c_a * x_a, buf_a, conv_a_w)
    y_a = b_a * conv_a * jax.nn.silu(z_a)

    qkv, new_buf_qkv = causal_dwconv(qkv, buf_qkv, conv_qkv_w)
    qkv = jax.nn.silu(qkv)
    q, k, v = jnp.split(qkv, 3, axis=-1)

    def heads(a):
        return a.reshape(bsz, t, N_HEADS_GDN, HEAD_DIM)

    q = l2norm(heads(q))
    k = l2norm(heads(k))
    v = heads(v)
    g = -jnp.exp(a_log.astype(f32)) * jax.nn.softplus(a_dec.astype(f32) + dt_bias.astype(f32))
    beta = jax.nn.sigmoid(b_beta.astype(f32))
    o, s_new = gated_delta_rule(q, k, v, g, beta, s0)
    o = rmsnorm(o, gdn_norm_w) * jax.nn.silu(heads(z_g).astype(f32))
    y_g = o.reshape(bsz, t, D_GDN).astype(x.dtype)

    mix = jnp.concatenate([y_a, y_g], axis=-1)
    out = jnp.einsum('btm,md->btd', mix, w_out)
    y = x + rmsnorm(out, ln_post)
    return y, new_buf_a, new_buf_qkv, s_new.astype(s0.dtype)


def setup_inputs(seed: int = 0) -> dict:
    key = jax.random.key(seed)
    ks = jax.random.split(key, 16)
    f32 = jnp.float32
    x_prompt = jax.random.normal(ks[0], (BATCH, SEQ, D_MODEL), f32)
    x_sample = jax.random.normal(ks[1], (DEC_BATCH, DEC_SEQ, D_MODEL), f32)
    cache_conv_a = jax.random.normal(ks[2], (DEPTH, DEC_BATCH, CONV_A_WIDTH - 1, D_CONV), f32)
    cache_conv_qkv = jax.random.normal(ks[3], (DEPTH, DEC_BATCH, CONV_QKV_WIDTH - 1, D_QKV), f32)
    state_gdn = 0.1 * jax.random.normal(ks[4], (DEPTH, DEC_BATCH, N_HEADS_GDN, HEAD_DIM, HEAD_DIM), f32)
    ln_pre = 1.0 + 0.01 * jax.random.normal(ks[5], (DEPTH, D_MODEL), f32)
    w_in = jax.random.normal(ks[6], (DEPTH, D_MODEL, D_PROJ), f32) * D_MODEL ** -0.5
    conv_a_w = jax.random.normal(ks[7], (DEPTH, CONV_A_WIDTH, D_CONV), f32) * CONV_A_WIDTH ** -0.5
    conv_qkv_w = jax.random.normal(ks[8], (DEPTH, CONV_QKV_WIDTH, D_QKV), f32) * CONV_QKV_WIDTH ** -0.5
    a_log = jnp.log(jax.random.uniform(ks[9], (DEPTH, N_HEADS_GDN), f32, 1.0, 16.0))
    dt_bias = 0.1 * jax.random.normal(ks[10], (DEPTH, N_HEADS_GDN), f32)
    gdn_norm_w = 1.0 + 0.01 * jax.random.normal(ks[11], (DEPTH, HEAD_DIM), f32)
    w_out = jax.random.normal(ks[12], (DEPTH, D_MIX, D_MODEL), f32) * D_MIX ** -0.5
    ln_post = 1.0 + 0.01 * jax.random.normal(ks[13], (DEPTH, D_MODEL), f32)
    return {"x_prompt": x_prompt, "x_sample": x_sample,
            "cache_conv_a": cache_conv_a, "cache_conv_qkv": cache_conv_qkv, "state_gdn": state_gdn,
            "ln_pre": ln_pre, "w_in": w_in, "conv_a_w": conv_a_w, "conv_qkv_w": conv_qkv_w,
            "a_log": a_log, "dt_bias": dt_bias, "gdn_norm_w": gdn_norm_w,
            "w_out": w_out, "ln_post": ln_post}


def reference(x_prompt, x_sample, cache_conv_a, cache_conv_qkv, state_gdn,
              ln_pre, w_in, conv_a_w, conv_qkv_w, a_log, dt_bias, gdn_norm_w, w_out, ln_post):
    y_p = x_prompt
    y_s = x_sample
    p_a, p_qkv, p_s = [], [], []
    s_a, s_qkv, s_s = [], [], []
    for layer in range(DEPTH):
        params = (ln_pre[layer], w_in[layer], conv_a_w[layer], conv_qkv_w[layer],
                  a_log[layer], dt_bias[layer], gdn_norm_w[layer], w_out[layer], ln_post[layer])
        bsz = y_p.shape[0]
        zero_a = jnp.zeros((bsz, CONV_A_WIDTH - 1, D_CONV), y_p.dtype)
        zero_qkv = jnp.zeros((bsz, CONV_QKV_WIDTH - 1, D_QKV), y_p.dtype)
        zero_s = jnp.zeros((bsz, N_HEADS_GDN, HEAD_DIM, HEAD_DIM), state_gdn.dtype)
        y_p, ba, bq, st = hybrid_layer(y_p, zero_a, zero_qkv, zero_s, *params)
        p_a.append(ba)
        p_qkv.append(bq)
        p_s.append(st)
        y_s, ba, bq, st = hybrid_layer(y_s, cache_conv_a[layer], cache_conv_qkv[layer],
                                       state_gdn[layer], *params)
        s_a.append(ba)
        s_qkv.append(bq)
        s_s.append(st)
    return (y_p, y_s, jnp.stack(p_a), jnp.stack(p_qkv), jnp.stack(p_s),
            jnp.stack(s_a), jnp.stack(s_qkv), jnp.stack(s_s))
```

```python
import functools
import math

import jax
import jax.numpy as jnp
from jax import lax
from jax.experimental import pallas as pl
from jax.experimental.pallas import tpu as pltpu

D_MODEL = 1024
D_CONV = 512
CONV_A_WIDTH = 3
N_HEADS = 4
HEAD_DIM = 128
D_GDN = N_HEADS * HEAD_DIM
D_QKV = 3 * D_GDN
CONV_QKV_WIDTH = 4
D_MIX = D_CONV + D_GDN
D_A = 4 * D_CONV
D_MAIN = D_A + D_QKV + D_GDN
CHUNK = 64
RMS_EPS = 1e-6
L2_EPS = 1e-6

LANES = 128
SUBLANES = 8
PAD = SUBLANES
GATE_W = 2 * LANES
PROJ_NBLK = 512
VMEM_LIMIT_BYTES = 52 * 1024 * 1024

F32 = jnp.float32
BF16 = jnp.bfloat16


def _dot(a, b):
    return jnp.dot(a, b, preferred_element_type=F32)


def _dot_nt(a, b):
    return lax.dot_general(a, b, (((1,), (1,)), ((), ())), preferred_element_type=F32)


def _dot_tn(a, b):
    return lax.dot_general(a, b, (((0,), (0,)), ((), ())), preferred_element_type=F32)


def _silu(z):
    return z * (1.0 / (1.0 + jnp.exp(-z)))


def _split3(x):
    hi = x.astype(BF16)
    r1 = x - hi.astype(F32)
    mid = r1.astype(BF16)
    lo = (r1 - mid.astype(F32)).astype(BF16)
    return jnp.concatenate([hi, mid, lo], axis=0)


def _unit_lower_inverse_minus_identity(a, c):
    levels = int(math.log2(c)) - 1
    ab = a.astype(BF16)
    x = _dot(ab, ab)
    p = -a
    for k in range(1, levels + 1):
        xb = x.astype(BF16)
        if k < levels:
            px = _dot(jnp.concatenate([p.astype(BF16), xb], axis=0), xb)
            p = p + x + px[:c]
            x = px[c:]
        else:
            p = p + x + _dot(p.astype(BF16), xb)
    return p


def _layer_kernel(*refs, tt, c, has_init):
    refs = list(refs)
    x_ref = refs.pop(0)
    if has_init:
        bufa_ref, bufq_ref, s0_ref = refs.pop(0), refs.pop(0), refs.pop(0)
    (lnpre_ref, win_ref, wab_ref, cwa_ref, cwq_ref, alog_ref, dtb_ref, gnw_ref, wout_ref,
     lnpost_ref, y_ref, na_ref, nq_ref, ns_ref,
     h_s, proj_s, ua_s, qc_s, qkv_s, zg_s, mix_s) = refs
    nc = tt // c
    t = pl.program_id(1)

    @pl.when(t == 0)
    def _load_stream_state():
        if has_init:
            ua_s[PAD - (CONV_A_WIDTH - 1):PAD, :] = bufa_ref[0, 0]
            qc_s[PAD - (CONV_QKV_WIDTH - 1):PAD, :] = bufq_ref[0, 0]
            ns_ref[0, 0] = s0_ref[0, 0]
        else:
            ua_s[0:PAD, :] = jnp.zeros((PAD, D_CONV), F32)
            qc_s[0:PAD, :] = jnp.zeros((PAD, D_QKV), F32)
            ns_ref[0, 0] = jnp.zeros((N_HEADS, HEAD_DIM, HEAD_DIM), F32)

    x = x_ref[0]
    var = jnp.mean(x * x, axis=-1, keepdims=True)
    h_s[...] = (x * lax.rsqrt(var + RMS_EPS) * lnpre_ref[...]).astype(BF16)

    for j in range(D_MAIN // PROJ_NBLK):
        cols = slice(j * PROJ_NBLK, (j + 1) * PROJ_NBLK)
        res = _dot(h_s[...], win_ref[:, cols])
        if j < D_A // PROJ_NBLK:
            proj_s[:, cols] = res
        elif j < (D_A + D_QKV) // PROJ_NBLK:
            qc_s[PAD:PAD + tt, j * PROJ_NBLK - D_A:(j + 1) * PROJ_NBLK - D_A] = res
        else:
            zg_s[...] = _silu(res)
    ab = _dot(h_s[...], wab_ref[...])

    for j in range(D_CONV // LANES):
        cs = slice(j * LANES, (j + 1) * LANES)
        ua_s[PAD:PAD + tt, cs] = (proj_s[:, D_CONV + j * LANES:D_CONV + (j + 1) * LANES]
                                  * proj_s[:, 2 * D_CONV + j * LANES:2 * D_CONV + (j + 1) * LANES])
    for j in range(D_CONV // LANES):
        cs = slice(j * LANES, (j + 1) * LANES)
        conv = cwa_ref[0:1, cs] * ua_s[PAD - 2:PAD - 2 + tt, cs]
        for k in range(1, CONV_A_WIDTH):
            conv = conv + cwa_ref[k:k + 1, cs] * ua_s[PAD - 2 + k:PAD - 2 + k + tt, cs]
        z = proj_s[:, 3 * D_CONV + j * LANES:3 * D_CONV + (j + 1) * LANES]
        mix_s[:, cs] = (proj_s[:, cs] * conv * _silu(z)).astype(BF16)
    tail_a = ua_s[PAD + tt - (CONV_A_WIDTH - 1):PAD + tt, :]
    ua_s[PAD - (CONV_A_WIDTH - 1):PAD, :] = tail_a
    na_ref[0, 0] = tail_a

    for j in range(D_QKV // LANES):
        cs = slice(j * LANES, (j + 1) * LANES)
        conv = cwq_ref[0:1, cs] * qc_s[PAD - 3:PAD - 3 + tt, cs]
        for k in range(1, CONV_QKV_WIDTH):
            conv = conv + cwq_ref[k:k + 1, cs] * qc_s[PAD - 3 + k:PAD - 3 + k + tt, cs]
        a = _silu(conv)
        if j < 2 * N_HEADS:
            rn = lax.rsqrt(jnp.sum(a * a, axis=-1, keepdims=True) + L2_EPS)
            if j < N_HEADS:
                rn = rn * (HEAD_DIM ** -0.5)
            a = a * rn
        qkv_s[:, cs] = a
    tail_q = qc_s[PAD + tt - (CONV_QKV_WIDTH - 1):PAD + tt, :]
    qc_s[PAD - (CONV_QKV_WIDTH - 1):PAD, :] = tail_q
    nq_ref[0, 0] = tail_q

    xs = ab[:, 0:LANES] + dtb_ref[...]
    softplus = jnp.maximum(xs, 0.0) + jnp.log1p(jnp.exp(-jnp.abs(xs)))
    g_t = -jnp.exp(alog_ref[...]) * softplus
    beta_t = 1.0 / (1.0 + jnp.exp(-ab[:, LANES:2 * LANES]))

    row = lax.broadcasted_iota(jnp.int32, (c, c), 0)
    col = lax.broadcasted_iota(jnp.int32, (c, c), 1)
    incl = row >= col
    strict = row > col
    row3 = lax.broadcasted_iota(jnp.int32, (c, 3 * c), 0)
    col3 = lax.broadcasted_iota(jnp.int32, (c, 3 * c), 1) & (c - 1)
    cumsum_mat = jnp.where(row3 >= col3, 1.0, 0.0).astype(BF16)

    for n in range(nc):
        rs = slice(n * c, (n + 1) * c)
        gb = _dot(cumsum_mat, _split3(g_t[rs]))
        gl = gb[c - 1:c, :]
        e_gb = jnp.exp(gb)
        e_rem = jnp.exp(gl - gb)
        e_gl = jnp.exp(gl)
        gb_rows = jnp.concatenate([gb, jnp.zeros((LANES - c, LANES), F32)], axis=0).T
        beta_c = beta_t[rs]
        for hd in range(N_HEADS):
            ls = slice(hd * HEAD_DIM, (hd + 1) * HEAD_DIM)
            qh = qkv_s[rs, ls]
            kh = qkv_s[rs, D_GDN + hd * HEAD_DIM:D_GDN + (hd + 1) * HEAD_DIM]
            vh = qkv_s[rs, 2 * D_GDN + hd * HEAD_DIM:2 * D_GDN + (hd + 1) * HEAD_DIM]
            bcol = beta_c[:, hd:hd + 1]
            ecol = e_gb[:, hd:hd + 1]
            diff = gb[:, hd:hd + 1] - gb_rows[hd:hd + 1, 0:c]
            decay = jnp.where(incl, jnp.exp(jnp.where(incl, diff, 0.0)), 0.0)
            kb = kh.astype(BF16)
            scores = _dot_nt(jnp.concatenate([kb, qh.astype(BF16)], axis=0), kb)
            a_mat = jnp.where(strict, scores[:c] * decay * bcol, 0.0)
            attn = scores[c:] * decay
            p = _unit_lower_inverse_minus_identity(a_mat, c)
            rhs = jnp.concatenate([vh * bcol, kh * (bcol * ecol)], axis=1)
            uw = rhs + _dot(p.astype(BF16), rhs.astype(BF16))
            u = uw[:, :HEAD_DIM]
            w = uw[:, HEAD_DIM:]
            s_old = ns_ref[0, 0, hd]
            wq = jnp.concatenate([w.astype(BF16), (qh * ecol).astype(BF16)], axis=0)
            wq_s = _dot(wq, s_old.astype(BF16))
            v_new = (u - wq_s[:c]).astype(BF16)
            o = wq_s[c:] + _dot(attn.astype(BF16), v_new)
            k_rem = (kh * e_rem[:, hd:hd + 1]).astype(BF16)
            ns_ref[0, 0, hd] = s_old * e_gl[:, hd:hd + 1] + _dot_tn(k_rem, v_new)
            o_var = jnp.mean(o * o, axis=-1, keepdims=True)
            o = o * lax.rsqrt(o_var + RMS_EPS) * gnw_ref[...]
            mix_s[rs, D_CONV + hd * HEAD_DIM:D_CONV + (hd + 1) * HEAD_DIM] = (
                o * zg_s[rs, ls]).astype(BF16)

    out = _dot(mix_s[...], wout_ref[...])
    out_var = jnp.mean(out * out, axis=-1, keepdims=True)
    y_ref[0] = x_ref[0] + out * lax.rsqrt(out_var + RMS_EPS) * lnpost_ref[...]


def _const_spec(shape):
    return pl.BlockSpec(shape, lambda b, t: (0,) * len(shape))


def _layer_call(x, init, params, *, tt, c, name):
    bsz, seq, _ = x.shape
    assert seq % tt == 0 and tt % c == 0 and tt % SUBLANES == 0 and c <= LANES
    has_init = init is not None
    stream_specs = [
        pl.BlockSpec((1, 1, CONV_A_WIDTH - 1, D_CONV), lambda b, t: (0, b, 0, 0)),
        pl.BlockSpec((1, 1, CONV_QKV_WIDTH - 1, D_QKV), lambda b, t: (0, b, 0, 0)),
        pl.BlockSpec((1, 1, N_HEADS, HEAD_DIM, HEAD_DIM), lambda b, t: (0, b, 0, 0, 0)),
    ]
    in_specs = [pl.BlockSpec((1, tt, D_MODEL), lambda b, t: (b, t, 0))]
    operands = [x]
    if has_init:
        in_specs += stream_specs
        operands += list(init)
    in_specs += [_const_spec(p.shape) for p in params]
    operands += list(params)
    out_shape = (
        jax.ShapeDtypeStruct(x.shape, x.dtype),
        jax.ShapeDtypeStruct((1, bsz, CONV_A_WIDTH - 1, D_CONV), F32),
        jax.ShapeDtypeStruct((1, bsz, CONV_QKV_WIDTH - 1, D_QKV), F32),
        jax.ShapeDtypeStruct((1, bsz, N_HEADS, HEAD_DIM, HEAD_DIM), F32),
    )
    out_specs = [pl.BlockSpec((1, tt, D_MODEL), lambda b, t: (b, t, 0))] + stream_specs
    scratch = [
        pltpu.VMEM((tt, D_MODEL), BF16),
        pltpu.VMEM((tt, D_A), F32),
        pltpu.VMEM((PAD + tt, D_CONV), F32),
        pltpu.VMEM((PAD + tt, D_QKV), F32),
        pltpu.VMEM((tt, D_QKV), F32),
        pltpu.VMEM((tt, D_GDN), F32),
        pltpu.VMEM((tt, D_MIX), BF16),
    ]
    return pl.pallas_call(
        functools.partial(_layer_kernel, tt=tt, c=c, has_init=has_init),
        grid=(bsz, seq // tt),
        in_specs=in_specs,
        out_specs=out_specs,
        out_shape=out_shape,
        scratch_shapes=scratch,
        compiler_params=pltpu.CompilerParams(
            dimension_semantics=("arbitrary", "arbitrary"),
            vmem_limit_bytes=VMEM_LIMIT_BYTES),
        name=name,
    )(*operands)


def _prepare_params(ln_pre, w_in, conv_a_w, conv_qkv_w, a_log, dt_bias, gdn_norm_w, w_out, ln_post):
    w_main = w_in[:, :D_MAIN].astype(BF16)
    w_gate = jnp.zeros((D_MODEL, GATE_W), F32)
    w_gate = w_gate.at[:, 0:N_HEADS].set(w_in[:, D_MAIN:D_MAIN + N_HEADS])
    w_gate = w_gate.at[:, LANES:LANES + N_HEADS].set(w_in[:, D_MAIN + N_HEADS:D_MAIN + 2 * N_HEADS])
    pad_heads = lambda v: jnp.zeros((1, LANES), F32).at[0, 0:N_HEADS].set(v)
    return (ln_pre.reshape(1, D_MODEL), w_main, w_gate.astype(BF16), conv_a_w, conv_qkv_w,
            pad_heads(a_log), pad_heads(dt_bias), gdn_norm_w.reshape(1, HEAD_DIM),
            w_out.astype(BF16), ln_post.reshape(1, D_MODEL))


def kernel(x_prompt, x_sample, cache_conv_a, cache_conv_qkv, state_gdn, ln_pre, w_in, conv_a_w,
           conv_qkv_w, a_log, dt_bias, gdn_norm_w, w_out, ln_post):
    assert ln_pre.shape[0] == 1, "single-layer kernel"
    params = _prepare_params(ln_pre[0], w_in[0], conv_a_w[0], conv_qkv_w[0], a_log[0], dt_bias[0],
                             gdn_norm_w[0], w_out[0], ln_post[0])
    t_p = x_prompt.shape[1]
    t_s = x_sample.shape[1]
    c_p = math.gcd(t_p, CHUNK)
    c_s = math.gcd(t_s, CHUNK)
    y_p, a_p, q_p, s_p = _layer_call(x_prompt, None, params, tt=min(t_p, 256), c=c_p,
                                     name="hybrid_prompt")
    y_s, a_s, q_s, s_s = _layer_call(x_sample, (cache_conv_a, cache_conv_qkv, state_gdn), params,
                                     tt=min(t_s, 256), c=c_s, name="hybrid_sample")
    return (y_p, y_s, a_p, q_p, s_p, a_s, q_s, s_s)
```

```python
import functools
import math

import jax
import jax.numpy as jnp
from jax import lax
from jax.experimental import pallas as pl
from jax.experimental.pallas import tpu as pltpu

D_MODEL = 1024
D_CONV = 512
CONV_A_WIDTH = 3
N_HEADS = 4
HEAD_DIM = 128
D_GDN = N_HEADS * HEAD_DIM
D_QKV = 3 * D_GDN
CONV_QKV_WIDTH = 4
D_MIX = D_CONV + D_GDN
D_A = 4 * D_CONV
D_MAIN = D_A + D_QKV + D_GDN
CHUNK = 64
RMS_EPS = 1e-6
L2_EPS = 1e-6

LANES = 128
SUBLANES = 8
PAD = SUBLANES
GATE_W = 2 * LANES
PROJ_NBLK = 512
VMEM_LIMIT_BYTES = 52 * 1024 * 1024

F32 = jnp.float32
BF16 = jnp.bfloat16


def _dot(a, b):
    return jnp.dot(a, b, preferred_element_type=F32)


def _dot_nt(a, b):
    return lax.dot_general(a, b, (((1,), (1,)), ((), ())), preferred_element_type=F32)


def _dot_tn(a, b):
    return lax.dot_general(a, b, (((0,), (0,)), ((), ())), preferred_element_type=F32)


def _silu(z):
    return z * (1.0 / (1.0 + jnp.exp(-z)))


def _split3(x):
    hi = x.astype(BF16)
    r1 = x - hi.astype(F32)
    mid = r1.astype(BF16)
    lo = (r1 - mid.astype(F32)).astype(BF16)
    return jnp.concatenate([hi, mid, lo], axis=0)


def _interleave(generators):
    alive = list(generators)
    while alive:
        still = []
        for g in alive:
            try:
                next(g)
                still.append(g)
            except StopIteration:
                pass
        alive = still


def _chunk_head_precompute(out, key, c, qh, kh, vh, bcol, ecol, rcol, diff, incl, strict):
    decay = jnp.where(incl, jnp.exp(jnp.where(incl, diff, 0.0)), 0.0)
    kb = kh.astype(BF16)
    scores = _dot_nt(jnp.concatenate([kb, qh.astype(BF16)], axis=0), kb)
    yield
    a_mat = jnp.where(strict, scores[:c] * decay * bcol, 0.0)
    attn = (scores[c:] * decay).astype(BF16)
    levels = int(math.log2(c)) - 1
    ab = a_mat.astype(BF16)
    x = _dot(ab, ab)
    yield
    p = -a_mat
    for k in range(1, levels + 1):
        xb = x.astype(BF16)
        if k < levels:
            px = _dot(jnp.concatenate([p.astype(BF16), xb], axis=0), xb)
            yield
            p = p + x + px[:c]
            x = px[c:]
        else:
            px = _dot(p.astype(BF16), xb)
            yield
            p = p + x + px
    rhs = jnp.concatenate([vh * bcol, kh * (bcol * ecol)], axis=1)
    uw = rhs + _dot(p.astype(BF16), rhs.astype(BF16))
    yield
    out[key] = dict(u=uw[:, :HEAD_DIM], w=uw[:, HEAD_DIM:].astype(BF16),
                    qg=(qh * ecol).astype(BF16), attn=attn, k_rem=(kh * rcol).astype(BF16))


def _chunk_head_recurrence(pre, state, hd, decay_last, finish):
    s_old = state[hd]
    c = pre["u"].shape[0]
    wq_s = _dot(jnp.concatenate([pre["w"], pre["qg"]], axis=0), s_old.astype(BF16))
    yield
    v_new = (pre["u"] - wq_s[:c]).astype(BF16)
    o = wq_s[c:] + _dot(pre["attn"], v_new)
    state[hd] = s_old * decay_last + _dot_tn(pre["k_rem"], v_new)
    yield
    finish(o)


def _layer_kernel(*refs, tt, c, has_init):
    refs = list(refs)
    x_ref = refs.pop(0)
    if has_init:
        bufa_ref, bufq_ref, s0_ref = refs.pop(0), refs.pop(0), refs.pop(0)
    (lnpre_ref, win_ref, wab_ref, cwa_ref, cwq_ref, alog_ref, dtb_ref, gnw_ref, wout_ref,
     lnpost_ref, y_ref, na_ref, nq_ref, ns_ref,
     h_s, proj_s, ua_s, qc_s, qkv_s, zg_s, mix_s) = refs
    nc = tt // c
    t = pl.program_id(1)

    @pl.when(t == 0)
    def _load_stream_state():
        if has_init:
            ua_s[PAD - (CONV_A_WIDTH - 1):PAD, :] = bufa_ref[0, 0]
            qc_s[PAD - (CONV_QKV_WIDTH - 1):PAD, :] = bufq_ref[0, 0]
            ns_ref[0, 0] = s0_ref[0, 0]
        else:
            ua_s[0:PAD, :] = jnp.zeros((PAD, D_CONV), F32)
            qc_s[0:PAD, :] = jnp.zeros((PAD, D_QKV), F32)
            ns_ref[0, 0] = jnp.zeros((N_HEADS, HEAD_DIM, HEAD_DIM), F32)

    x = x_ref[0]
    var = jnp.mean(x * x, axis=-1, keepdims=True)
    h_s[...] = (x * lax.rsqrt(var + RMS_EPS) * lnpre_ref[...]).astype(BF16)

    for j in range(D_MAIN // PROJ_NBLK):
        cols = slice(j * PROJ_NBLK, (j + 1) * PROJ_NBLK)
        res = _dot(h_s[...], win_ref[:, cols])
        if j < D_A // PROJ_NBLK:
            proj_s[:, cols] = res
        elif j < (D_A + D_QKV) // PROJ_NBLK:
            qc_s[PAD:PAD + tt, j * PROJ_NBLK - D_A:(j + 1) * PROJ_NBLK - D_A] = res
        else:
            zg_s[...] = _silu(res)
    ab = _dot(h_s[...], wab_ref[...])

    for j in range(D_CONV // LANES):
        cs = slice(j * LANES, (j + 1) * LANES)
        ua_s[PAD:PAD + tt, cs] = (proj_s[:, D_CONV + j * LANES:D_CONV + (j + 1) * LANES]
                                  * proj_s[:, 2 * D_CONV + j * LANES:2 * D_CONV + (j + 1) * LANES])
    for j in range(D_CONV // LANES):
        cs = slice(j * LANES, (j + 1) * LANES)
        conv = cwa_ref[0:1, cs] * ua_s[PAD - 2:PAD - 2 + tt, cs]
        for k in range(1, CONV_A_WIDTH):
            conv = conv + cwa_ref[k:k + 1, cs] * ua_s[PAD - 2 + k:PAD - 2 + k + tt, cs]
        z = proj_s[:, 3 * D_CONV + j * LANES:3 * D_CONV + (j + 1) * LANES]
        mix_s[:, cs] = (proj_s[:, cs] * conv * _silu(z)).astype(BF16)
    tail_a = ua_s[PAD + tt - (CONV_A_WIDTH - 1):PAD + tt, :]
    ua_s[PAD - (CONV_A_WIDTH - 1):PAD, :] = tail_a
    na_ref[0, 0] = tail_a

    for j in range(D_QKV // LANES):
        cs = slice(j * LANES, (j + 1) * LANES)
        conv = cwq_ref[0:1, cs] * qc_s[PAD - 3:PAD - 3 + tt, cs]
        for k in range(1, CONV_QKV_WIDTH):
            conv = conv + cwq_ref[k:k + 1, cs] * qc_s[PAD - 3 + k:PAD - 3 + k + tt, cs]
        a = _silu(conv)
        if j < 2 * N_HEADS:
            rn = lax.rsqrt(jnp.sum(a * a, axis=-1, keepdims=True) + L2_EPS)
            if j < N_HEADS:
                rn = rn * (HEAD_DIM ** -0.5)
            a = a * rn
        qkv_s[:, cs] = a
    tail_q = qc_s[PAD + tt - (CONV_QKV_WIDTH - 1):PAD + tt, :]
    qc_s[PAD - (CONV_QKV_WIDTH - 1):PAD, :] = tail_q
    nq_ref[0, 0] = tail_q

    xs = ab[:, 0:LANES] + dtb_ref[...]
    softplus = jnp.maximum(xs, 0.0) + jnp.log1p(jnp.exp(-jnp.abs(xs)))
    g_t = -jnp.exp(alog_ref[...]) * softplus
    beta_t = 1.0 / (1.0 + jnp.exp(-ab[:, LANES:2 * LANES]))

    row = lax.broadcasted_iota(jnp.int32, (c, c), 0)
    col = lax.broadcasted_iota(jnp.int32, (c, c), 1)
    incl = row >= col
    strict = row > col
    row3 = lax.broadcasted_iota(jnp.int32, (c, 3 * c), 0)
    col3 = lax.broadcasted_iota(jnp.int32, (c, 3 * c), 1) & (c - 1)
    cumsum_mat = jnp.where(row3 >= col3, 1.0, 0.0).astype(BF16)

    pre = {}
    decay_last = {}
    precompute = []
    for n in range(nc):
        rs = slice(n * c, (n + 1) * c)
        gb = _dot(cumsum_mat, _split3(g_t[rs]))
        gl = gb[c - 1:c, :]
        e_gb = jnp.exp(gb)
        e_rem = jnp.exp(gl - gb)
        e_gl = jnp.exp(gl)
        gb_rows = jnp.concatenate([gb, jnp.zeros((LANES - c, LANES), F32)], axis=0).T
        beta_c = beta_t[rs]
        for hd in range(N_HEADS):
            qh = qkv_s[rs, hd * HEAD_DIM:(hd + 1) * HEAD_DIM]
            kh = qkv_s[rs, D_GDN + hd * HEAD_DIM:D_GDN + (hd + 1) * HEAD_DIM]
            vh = qkv_s[rs, 2 * D_GDN + hd * HEAD_DIM:2 * D_GDN + (hd + 1) * HEAD_DIM]
            diff = gb[:, hd:hd + 1] - gb_rows[hd:hd + 1, 0:c]
            decay_last[n, hd] = e_gl[:, hd:hd + 1]
            precompute.append(_chunk_head_precompute(
                pre, (n, hd), c, qh, kh, vh, beta_c[:, hd:hd + 1], e_gb[:, hd:hd + 1],
                e_rem[:, hd:hd + 1], diff, incl, strict))
    _interleave(precompute)

    def finish_head(n, hd):
        def finish(o):
            rs = slice(n * c, (n + 1) * c)
            o_var = jnp.mean(o * o, axis=-1, keepdims=True)
            o = o * lax.rsqrt(o_var + RMS_EPS) * gnw_ref[...]
            mix_s[rs, D_CONV + hd * HEAD_DIM:D_CONV + (hd + 1) * HEAD_DIM] = (
                o * zg_s[rs, hd * HEAD_DIM:(hd + 1) * HEAD_DIM]).astype(BF16)
        return finish

    state = [ns_ref[0, 0, hd] for hd in range(N_HEADS)]
    for n in range(nc):
        _interleave([_chunk_head_recurrence(pre[n, hd], state, hd, decay_last[n, hd],
                                            finish_head(n, hd)) for hd in range(N_HEADS)])
    for hd in range(N_HEADS):
        ns_ref[0, 0, hd] = state[hd]

    out = _dot(mix_s[...], wout_ref[...])
    out_var = jnp.mean(out * out, axis=-1, keepdims=True)
    y_ref[0] = x_ref[0] + out * lax.rsqrt(out_var + RMS_EPS) * lnpost_ref[...]


def _const_spec(shape):
    return pl.BlockSpec(shape, lambda b, t: (0,) * len(shape))


def _layer_call(x, init, params, *, tt, c, name):
    bsz, seq, _ = x.shape
    assert seq % tt == 0 and tt % c == 0 and tt % SUBLANES == 0 and c <= LANES
    has_init = init is not None
    stream_specs = [
        pl.BlockSpec((1, 1, CONV_A_WIDTH - 1, D_CONV), lambda b, t: (0, b, 0, 0)),
        pl.BlockSpec((1, 1, CONV_QKV_WIDTH - 1, D_QKV), lambda b, t: (0, b, 0, 0)),
        pl.BlockSpec((1, 1, N_HEADS, HEAD_DIM, HEAD_DIM), lambda b, t: (0, b, 0, 0, 0)),
    ]
    in_specs = [pl.BlockSpec((1, tt, D_MODEL), lambda b, t: (b, t, 0))]
    operands = [x]
    if has_init:
        in_specs += stream_specs
        operands += list(init)
    in_specs += [_const_spec(p.shape) for p in params]
    operands += list(params)
    out_shape = (
        jax.ShapeDtypeStruct(x.shape, x.dtype),
        jax.ShapeDtypeStruct((1, bsz, CONV_A_WIDTH - 1, D_CONV), F32),
        jax.ShapeDtypeStruct((1, bsz, CONV_QKV_WIDTH - 1, D_QKV), F32),
        jax.ShapeDtypeStruct((1, bsz, N_HEADS, HEAD_DIM, HEAD_DIM), F32),
    )
    out_specs = [pl.BlockSpec((1, tt, D_MODEL), lambda b, t: (b, t, 0))] + stream_specs
    scratch = [
        pltpu.VMEM((tt, D_MODEL), BF16),
        pltpu.VMEM((tt, D_A), F32),
        pltpu.VMEM((PAD + tt, D_CONV), F32),
        pltpu.VMEM((PAD + tt, D_QKV), F32),
        pltpu.VMEM((tt, D_QKV), F32),
        pltpu.VMEM((tt, D_GDN), F32),
        pltpu.VMEM((tt, D_MIX), BF16),
    ]
    return pl.pallas_call(
        functools.partial(_layer_kernel, tt=tt, c=c, has_init=has_init),
        grid=(bsz, seq // tt),
        in_specs=in_specs,
        out_specs=out_specs,
        out_shape=out_shape,
        scratch_shapes=scratch,
        compiler_params=pltpu.CompilerParams(
            dimension_semantics=("arbitrary", "arbitrary"),
            vmem_limit_bytes=VMEM_LIMIT_BYTES),
        name=name,
    )(*operands)


def _prepare_params(ln_pre, w_in, conv_a_w, conv_qkv_w, a_log, dt_bias, gdn_norm_w, w_out, ln_post):
    w_main = w_in[:, :D_MAIN].astype(BF16)
    w_gate = jnp.zeros((D_MODEL, GATE_W), F32)
    w_gate = w_gate.at[:, 0:N_HEADS].set(w_in[:, D_MAIN:D_MAIN + N_HEADS])
    w_gate = w_gate.at[:, LANES:LANES + N_HEADS].set(w_in[:, D_MAIN + N_HEADS:D_MAIN + 2 * N_HEADS])
    pad_heads = lambda v: jnp.zeros((1, LANES), F32).at[0, 0:N_HEADS].set(v)
    return (ln_pre.reshape(1, D_MODEL), w_main, w_gate.astype(BF16), conv_a_w, conv_qkv_w,
            pad_heads(a_log), pad_heads(dt_bias), gdn_norm_w.reshape(1, HEAD_DIM),
            w_out.astype(BF16), ln_post.reshape(1, D_MODEL))


def kernel(x_prompt, x_sample, cache_conv_a, cache_conv_qkv, state_gdn, ln_pre, w_in, conv_a_w,
           conv_qkv_w, a_log, dt_bias, gdn_norm_w, w_out, ln_post):
    assert ln_pre.shape[0] == 1, "single-layer kernel"
    params = _prepare_params(ln_pre[0], w_in[0], conv_a_w[0], conv_qkv_w[0], a_log[0], dt_bias[0],
                             gdn_norm_w[0], w_out[0], ln_post[0])
    t_p = x_prompt.shape[1]
    t_s = x_sample.shape[1]
    c_p = math.gcd(t_p, CHUNK)
    c_s = math.gcd(t_s, CHUNK)
    y_p, a_p, q_p, s_p = _layer_call(x_prompt, None, params, tt=min(t_p, 256), c=c_p,
                                     name="hybrid_prompt")
    y_s, a_s, q_s, s_s = _layer_call(x_sample, (cache_conv_a, cache_conv_qkv, state_gdn), params,
                                     tt=min(t_s, 256), c=c_s, name="hybrid_sample")
    return (y_p, y_s, a_p, q_p, s_p, a_s, q_s, s_s)
```

```python
import functools
import math

import jax
import jax.numpy as jnp
from jax import lax
from jax.experimental import pallas as pl
from jax.experimental.pallas import tpu as pltpu

D_MODEL = 1024
D_CONV = 512
CONV_A_WIDTH = 3
N_HEADS = 4
HEAD_DIM = 128
D_GDN = N_HEADS * HEAD_DIM
D_QKV = 3 * D_GDN
CONV_QKV_WIDTH = 4
D_MIX = D_CONV + D_GDN
D_A = 4 * D_CONV
D_MAIN = D_A + D_QKV + D_GDN
CHUNK = 64
RMS_EPS = 1e-6
L2_EPS = 1e-6

LANES = 128
SUBLANES = 8
PAD = SUBLANES
GATE_W = 2 * LANES
PROJ_NBLK = 256
TILE_T = 256
VMEM_LIMIT_BYTES = 52 * 1024 * 1024

EARLY_PIECES = (16, 2, 3, 4, 5, 0, 6, 1, 7)
EARLY_FOLLOWERS = (("gates",), ("conv_in",), ("a", 0), ("a", 1), ("a", 2), ("a", 3))
_QKV_PIECE = lambda m: (D_A // PROJ_NBLK + m, (("b", 2 * m), ("b", 2 * m + 1)))
LATE_FIRST = (_QKV_PIECE(0), _QKV_PIECE(2))
LATE_REST = (_QKV_PIECE(1), _QKV_PIECE(3), _QKV_PIECE(4), _QKV_PIECE(5),
             (14, ()), (15, (("silu_zg",),)))
HEAD_GROUPS = ((0, (0, 1)), (2, (2, 3)))

F32 = jnp.float32
BF16 = jnp.bfloat16


def _dot(a, b):
    return jnp.dot(a, b, preferred_element_type=F32)


def _dot_nt(a, b):
    return lax.dot_general(a, b, (((1,), (1,)), ((), ())), preferred_element_type=F32)


def _dot_tn(a, b):
    return lax.dot_general(a, b, (((0,), (0,)), ((), ())), preferred_element_type=F32)


def _silu(z):
    h = 0.5 * z
    return h + h * jnp.tanh(h)


def _split3(x):
    hi = x.astype(BF16)
    r1 = x - hi.astype(F32)
    mid = r1.astype(BF16)
    lo = (r1 - mid.astype(F32)).astype(BF16)
    return jnp.concatenate([hi, mid, lo], axis=0)


def _round_robin(generators):
    alive = list(generators)
    while alive:
        still = []
        for g in alive:
            try:
                next(g)
                still.append(g)
            except StopIteration:
                pass
        alive = still
        yield


def _interleave(generators):
    for _ in _round_robin(generators):
        pass


def _chunk_head_precompute(out, key, c, qh, kh, load_v, bcol, ecol, rcol, diff, incl, strict, eye):
    decay = jnp.exp(diff + incl)
    decay_strict = jnp.exp(diff + strict)
    kb = kh.astype(BF16)
    scores = _dot_nt(jnp.concatenate([kb, qh.astype(BF16)], axis=0), kb)
    yield
    a_mat = scores[:c] * decay_strict * bcol
    attn = (scores[c:] * decay).astype(BF16)
    levels = int(math.log2(c)) - 1
    ab = a_mat.astype(BF16)
    x = _dot(ab, ab)
    yield
    t = eye - a_mat
    for k in range(1, levels + 1):
        xb = x.astype(BF16)
        if k < levels:
            tx = _dot(jnp.concatenate([t.astype(BF16), xb], axis=0), xb)
            yield
            t = t + tx[:c]
            x = tx[c:]
        else:
            tx = _dot(t.astype(BF16), xb)
            yield
            t = t + tx
    rhs = jnp.concatenate([load_v() * bcol, kh * (bcol * ecol)], axis=1)
    uw = _dot(t.astype(BF16), rhs.astype(BF16))
    yield
    out[key] = dict(u=uw[:, :HEAD_DIM], w=uw[:, HEAD_DIM:].astype(BF16),
                    qg=(qh * ecol).astype(BF16), attn=attn, k_rem=(kh * rcol).astype(BF16))


def _chunk_head_recurrence(pre, state, hd, decay_last, finish):
    s_old = state[hd]
    c = pre["u"].shape[0]
    wq_s = _dot(jnp.concatenate([pre["w"], pre["qg"]], axis=0), s_old.astype(BF16))
    yield
    v_new = (pre["u"] - wq_s[:c]).astype(BF16)
    o = wq_s[c:] + _dot(pre["attn"], v_new)
    state[hd] = s_old * decay_last + _dot_tn(pre["k_rem"], v_new)
    yield
    finish(o)


def _normalised_input(x_ref, lnpre_ref):
    x = x_ref[0]
    var = jnp.mean(x * x, axis=-1, keepdims=True)
    return (x * lax.rsqrt(var + RMS_EPS) * lnpre_ref[...]).astype(BF16)


def _project_piece(tt, j, hb, win_ref, proj_s, qc_s, zraw_s, gate_s):
    lo = j * PROJ_NBLK
    hi = (j + 1) * PROJ_NBLK
    res = _dot(hb, win_ref[:, lo:hi])
    if lo < D_A:
        proj_s[:, lo:hi] = res
    elif lo < D_A + D_QKV:
        qc_s[PAD:PAD + tt, lo - D_A:hi - D_A] = res
    elif lo < D_MAIN:
        zraw_s[:, lo - D_A - D_QKV:hi - D_A - D_QKV] = res
    else:
        gate_s[...] = res


def _elementwise_pieces(tt, cwa_ref, cwq_ref, alog_ref, dtb_ref, proj_s, ua_s, qc_s, zraw_s, gate_s,
                        qkv_s, zg_s, mix_s, g_s, beta_s):
    def conv_in():
        for j in range(D_CONV // LANES):
            cs = slice(j * LANES, (j + 1) * LANES)
            ua_s[PAD:PAD + tt, cs] = (
                proj_s[:, D_CONV + j * LANES:D_CONV + (j + 1) * LANES]
                * proj_s[:, 2 * D_CONV + j * LANES:2 * D_CONV + (j + 1) * LANES])

    def group_a(j):
        cs = slice(j * LANES, (j + 1) * LANES)
        conv = cwa_ref[0:1, cs] * ua_s[PAD - 2:PAD - 2 + tt, cs]
        for k in range(1, CONV_A_WIDTH):
            conv = conv + cwa_ref[k:k + 1, cs] * ua_s[PAD - 2 + k:PAD - 2 + k + tt, cs]
        z = proj_s[:, 3 * D_CONV + j * LANES:3 * D_CONV + (j + 1) * LANES]
        mix_s[:, cs] = (proj_s[:, cs] * conv * _silu(z)).astype(BF16)

    def group_b(j):
        cs = slice(j * LANES, (j + 1) * LANES)
        conv = cwq_ref[0:1, cs] * qc_s[PAD - 3:PAD - 3 + tt, cs]
        for k in range(1, CONV_QKV_WIDTH):
            conv = conv + cwq_ref[k:k + 1, cs] * qc_s[PAD - 3 + k:PAD - 3 + k + tt, cs]
        a = _silu(conv)
        if j < 2 * N_HEADS:
            rn = lax.rsqrt(jnp.sum(a * a, axis=-1, keepdims=True) + L2_EPS)
            if j < N_HEADS:
                rn = rn * (HEAD_DIM ** -0.5)
            a = a * rn
        qkv_s[:, cs] = a

    def gates():
        xs = gate_s[:, 0:LANES] + dtb_ref[...]
        softplus = jnp.maximum(xs, 0.0) + jnp.log1p(jnp.exp(-jnp.abs(xs)))
        g_s[...] = -jnp.exp(alog_ref[...]) * softplus
        beta_s[...] = 1.0 / (1.0 + jnp.exp(-gate_s[:, LANES:2 * LANES]))

    def silu_zg():
        zg_s[...] = _silu(zraw_s[...])

    return {"gates": gates, "conv_in": conv_in, "a": group_a, "b": group_b, "silu_zg": silu_zg}


def _layer_kernel(*refs, tt, c, n_t, has_init):
    refs = list(refs)
    x_ref, xnext_ref = refs.pop(0), refs.pop(0)
    if has_init:
        bufa_ref, bufq_ref, s0_ref = refs.pop(0), refs.pop(0), refs.pop(0)
    (lnpre_ref, win_ref, cwa_ref, cwq_ref, alog_ref, dtb_ref, gnw_ref, wout_ref, lnpost_ref,
     y_ref, na_ref, nq_ref, ns_ref,
     h_s, proj_s, ua_s, qc_s, zraw_s, gate_s, qkv_s, zg_s, g_s, beta_s, mix_s) = refs
    nc = tt // c
    s = pl.program_id(0)
    proj_bufs = (win_ref, proj_s, qc_s, zraw_s, gate_s)
    pieces = _elementwise_pieces(tt, cwa_ref, cwq_ref, alog_ref, dtb_ref, proj_s, ua_s, qc_s,
                                 zraw_s, gate_s, qkv_s, zg_s, mix_s, g_s, beta_s)

    @pl.when(s == 0)
    def _first_tile_early_pieces():
        hb = _normalised_input(x_ref, lnpre_ref)
        h_s[...] = hb
        for j in EARLY_PIECES:
            _project_piece(tt, j, hb, *proj_bufs)

    @pl.when(s % n_t == 0)
    def _load_stream_state():
        if has_init:
            ua_s[PAD - (CONV_A_WIDTH - 1):PAD, :] = bufa_ref[0, 0]
            qc_s[PAD - (CONV_QKV_WIDTH - 1):PAD, :] = bufq_ref[0, 0]
            ns_ref[0, 0] = s0_ref[0, 0]
        else:
            ua_s[0:PAD, :] = jnp.zeros((PAD, D_CONV), F32)
            qc_s[0:PAD, :] = jnp.zeros((PAD, D_QKV), F32)
            ns_ref[0, 0] = jnp.zeros((N_HEADS, HEAD_DIM, HEAD_DIM), F32)

    hb = h_s[...]
    for kind, *args in EARLY_FOLLOWERS:
        pieces[kind](*args)

    def project_pieces(schedule):
        for j, followers in schedule:
            _project_piece(tt, j, hb, *proj_bufs)
            for kind, *args in followers:
                pieces[kind](*args)
            yield

    _interleave([project_pieces(LATE_FIRST)])

    row = lax.broadcasted_iota(jnp.int32, (c, c), 0)
    col = lax.broadcasted_iota(jnp.int32, (c, c), 1)
    incl = jnp.where(row >= col, 0.0, -jnp.inf)
    strict = jnp.where(row > col, 0.0, -jnp.inf)
    eye = jnp.where(row == col, 1.0, 0.0)
    row3 = lax.broadcasted_iota(jnp.int32, (c, 3 * c), 0)
    col3 = lax.broadcasted_iota(jnp.int32, (c, 3 * c), 1) & (c - 1)
    cumsum_mat = jnp.where(row3 >= col3, 1.0, 0.0).astype(BF16)

    pre = {}
    decay_last = {}
    gate_terms = []
    for n in range(nc):
        rs = slice(n * c, (n + 1) * c)
        gb = _dot(cumsum_mat, _split3(g_s[rs, :]))
        gl = gb[c - 1:c, :]
        gb_rows = jnp.concatenate([gb, jnp.zeros((LANES - c, LANES), F32)], axis=0).T
        gate_terms.append((gb, gb_rows, jnp.exp(gb), jnp.exp(gl - gb), jnp.exp(gl), beta_s[rs, :]))

    def head_group_chains(start_round, heads):
        for _ in range(start_round):
            yield
        chains = []
        for n in range(nc):
            rs = slice(n * c, (n + 1) * c)
            gb, gb_rows, e_gb, e_rem, e_gl, beta_c = gate_terms[n]
            for hd in heads:
                qh = qkv_s[rs, hd * HEAD_DIM:(hd + 1) * HEAD_DIM]
                kh = qkv_s[rs, D_GDN + hd * HEAD_DIM:D_GDN + (hd + 1) * HEAD_DIM]
                load_v = functools.partial(
                    lambda rows, h: qkv_s[rows, 2 * D_GDN + h * HEAD_DIM:2 * D_GDN + (h + 1) * HEAD_DIM],
                    rs, hd)
                diff = gb[:, hd:hd + 1] - gb_rows[hd:hd + 1, 0:c]
                decay_last[n, hd] = e_gl[:, hd:hd + 1]
                chains.append(_chunk_head_precompute(
                    pre, (n, hd), c, qh, kh, load_v, beta_c[:, hd:hd + 1], e_gb[:, hd:hd + 1],
                    e_rem[:, hd:hd + 1], diff, incl, strict, eye))
        yield from _round_robin(chains)

    _interleave([project_pieces(LATE_REST)]
                + [head_group_chains(start, heads) for start, heads in HEAD_GROUPS])
    tail_a = ua_s[PAD + tt - (CONV_A_WIDTH - 1):PAD + tt, :]
    ua_s[PAD - (CONV_A_WIDTH - 1):PAD, :] = tail_a
    na_ref[0, 0] = tail_a
    tail_q = qc_s[PAD + tt - (CONV_QKV_WIDTH - 1):PAD + tt, :]
    qc_s[PAD - (CONV_QKV_WIDTH - 1):PAD, :] = tail_q
    nq_ref[0, 0] = tail_q

    def finish_head(n, hd):
        def finish(o):
            rs = slice(n * c, (n + 1) * c)
            o_var = jnp.mean(o * o, axis=-1, keepdims=True)
            o = o * lax.rsqrt(o_var + RMS_EPS) * gnw_ref[...]
            mix_s[rs, D_CONV + hd * HEAD_DIM:D_CONV + (hd + 1) * HEAD_DIM] = (
                o * zg_s[rs, hd * HEAD_DIM:(hd + 1) * HEAD_DIM]).astype(BF16)
        return finish

    state = [ns_ref[0, 0, hd] for hd in range(N_HEADS)]

    def recurrence():
        for n in range(nc):
            yield from _round_robin([
                _chunk_head_recurrence(pre[n, hd], state, hd, decay_last[n, hd], finish_head(n, hd))
                for hd in range(N_HEADS)])

    def next_tile_early_pieces():
        hb_next = _normalised_input(xnext_ref, lnpre_ref)
        h_s[...] = hb_next
        for j in EARLY_PIECES:
            _project_piece(tt, j, hb_next, *proj_bufs)
            yield

    _interleave([recurrence(), next_tile_early_pieces()])
    for hd in range(N_HEADS):
        ns_ref[0, 0, hd] = state[hd]

    out = _dot(mix_s[...], wout_ref[...])
    out_var = jnp.mean(out * out, axis=-1, keepdims=True)
    y_ref[0] = x_ref[0] + out * lax.rsqrt(out_var + RMS_EPS) * lnpost_ref[...]


def _layer_call(x, init, params, *, tt, c, name):
    bsz, seq, _ = x.shape
    assert seq % tt == 0 and tt % c == 0 and tt % SUBLANES == 0 and c <= LANES
    has_init = init is not None
    n_t = seq // tt
    n_tiles = bsz * n_t

    def this_tile(s):
        return s

    def next_tile(s):
        return jnp.minimum(s + 1, n_tiles - 1)

    tile_spec = lambda tile: pl.BlockSpec((1, tt, D_MODEL), lambda s: (tile(s) // n_t, tile(s) % n_t, 0))
    stream_specs = [
        pl.BlockSpec((1, 1, CONV_A_WIDTH - 1, D_CONV), lambda s: (0, s // n_t, 0, 0)),
        pl.BlockSpec((1, 1, CONV_QKV_WIDTH - 1, D_QKV), lambda s: (0, s // n_t, 0, 0)),
        pl.BlockSpec((1, 1, N_HEADS, HEAD_DIM, HEAD_DIM), lambda s: (0, s // n_t, 0, 0, 0)),
    ]
    in_specs = [tile_spec(this_tile), tile_spec(next_tile)]
    operands = [x, x]
    if has_init:
        in_specs += stream_specs
        operands += list(init)
    in_specs += [pl.BlockSpec(p.shape, lambda s, nd=p.ndim: (0,) * nd) for p in params]
    operands += list(params)
    out_shape = (
        jax.ShapeDtypeStruct(x.shape, x.dtype),
        jax.ShapeDtypeStruct((1, bsz, CONV_A_WIDTH - 1, D_CONV), F32),
        jax.ShapeDtypeStruct((1, bsz, CONV_QKV_WIDTH - 1, D_QKV), F32),
        jax.ShapeDtypeStruct((1, bsz, N_HEADS, HEAD_DIM, HEAD_DIM), F32),
    )
    out_specs = [tile_spec(this_tile)] + stream_specs
    scratch = [
        pltpu.VMEM((tt, D_MODEL), BF16),
        pltpu.VMEM((tt, D_A), F32),
        pltpu.VMEM((PAD + tt, D_CONV), F32),
        pltpu.VMEM((PAD + tt, D_QKV), F32),
        pltpu.VMEM((tt, D_GDN), F32),
        pltpu.VMEM((tt, GATE_W), F32),
        pltpu.VMEM((tt, D_QKV), F32),
        pltpu.VMEM((tt, D_GDN), F32),
        pltpu.VMEM((tt, LANES), F32),
        pltpu.VMEM((tt, LANES), F32),
        pltpu.VMEM((tt, D_MIX), BF16),
    ]
    return pl.pallas_call(
        functools.partial(_layer_kernel, tt=tt, c=c, n_t=n_t, has_init=has_init),
        grid=(n_tiles,),
        in_specs=in_specs,
        out_specs=out_specs,
        out_shape=out_shape,
        scratch_shapes=scratch,
        compiler_params=pltpu.CompilerParams(
            dimension_semantics=("arbitrary",),
            vmem_limit_bytes=VMEM_LIMIT_BYTES),
        name=name,
    )(*operands)


def _prepare_params(ln_pre, w_in, conv_a_w, conv_qkv_w, a_log, dt_bias, gdn_norm_w, w_out, ln_post):
    w_all = jnp.zeros((D_MODEL, D_MAIN + GATE_W), F32)
    w_all = w_all.at[:, 0:D_MAIN].set(w_in[:, :D_MAIN])
    w_all = w_all.at[:, D_MAIN:D_MAIN + N_HEADS].set(w_in[:, D_MAIN:D_MAIN + N_HEADS])
    w_all = w_all.at[:, D_MAIN + LANES:D_MAIN + LANES + N_HEADS].set(
        w_in[:, D_MAIN + N_HEADS:D_MAIN + 2 * N_HEADS])
    pad_heads = lambda v: jnp.zeros((1, LANES), F32).at[0, 0:N_HEADS].set(v)
    return (ln_pre.reshape(1, D_MODEL), w_all.astype(BF16), conv_a_w, conv_qkv_w,
            pad_heads(a_log), pad_heads(dt_bias), gdn_norm_w.reshape(1, HEAD_DIM),
            w_out.astype(BF16), ln_post.reshape(1, D_MODEL))


def kernel(x_prompt, x_sample, cache_conv_a, cache_conv_qkv, state_gdn, ln_pre, w_in, conv_a_w,
           conv_qkv_w, a_log, dt_bias, gdn_norm_w, w_out, ln_post):
    assert ln_pre.shape[0] == 1, "single-layer kernel"
    params = _prepare_params(ln_pre[0], w_in[0], conv_a_w[0], conv_qkv_w[0], a_log[0], dt_bias[0],
                             gdn_norm_w[0], w_out[0], ln_post[0])
    t_p = x_prompt.shape[1]
    t_s = x_sample.shape[1]
    y_p, a_p, q_p, s_p = _layer_call(x_prompt, None, params, tt=min(t_p, TILE_T),
                                     c=math.gcd(t_p, CHUNK), name="hybrid_prompt")
    y_s, a_s, q_s, s_s = _layer_call(x_sample, (cache_conv_a, cache_conv_qkv, state_gdn), params,
                                     tt=min(t_s, TILE_T), c=math.gcd(t_s, CHUNK),
                                     name="hybrid_sample")
    return (y_p, y_s, a_p, q_p, s_p, a_s, q_s, s_s)
```

```python
import functools
import math

import jax
import jax.numpy as jnp
from jax import lax
from jax.experimental import pallas as pl
from jax.experimental.pallas import tpu as pltpu

D_MODEL = 1024
D_CONV = 512
CONV_A_WIDTH = 3
N_HEADS = 4
HEAD_DIM = 128
D_GDN = N_HEADS * HEAD_DIM
D_QKV = 3 * D_GDN
CONV_QKV_WIDTH = 4
D_MIX = D_CONV + D_GDN
D_A = 4 * D_CONV
D_MAIN = D_A + D_QKV + D_GDN
CHUNK = 64
RMS_EPS = 1e-6
L2_EPS = 1e-6

LANES = 128
SUBLANES = 8
PAD = SUBLANES
GATE_W = 2 * LANES
PROJ_NBLK = 256
TILE_ROWS = 256
VMEM_LIMIT_BYTES = 52 * 1024 * 1024

EARLY_PIECES = (16, 2, 3, 4, 5, 0, 6, 1, 7)
EARLY_FOLLOWERS = (("gates",), ("conv_in",), ("a", 0), ("a", 1), ("a", 2), ("a", 3))
_QKV_PIECE = lambda m: (D_A // PROJ_NBLK + m, (("b", 2 * m), ("b", 2 * m + 1)))
LATE_FIRST = (_QKV_PIECE(0), _QKV_PIECE(2))
LATE_REST = (_QKV_PIECE(1), _QKV_PIECE(3), _QKV_PIECE(4), _QKV_PIECE(5),
             (14, ()), (15, (("silu_zg",),)))
HEAD_GROUPS = ((0, (0, 1)), (2, (2, 3)))

F32 = jnp.float32
BF16 = jnp.bfloat16


def _dot(a, b):
    return jnp.dot(a, b, preferred_element_type=F32)


def _dot_nt(a, b):
    return lax.dot_general(a, b, (((1,), (1,)), ((), ())), preferred_element_type=F32)


def _dot_tn(a, b):
    return lax.dot_general(a, b, (((0,), (0,)), ((), ())), preferred_element_type=F32)


def _silu(z):
    h = 0.5 * z
    return h + h * jnp.tanh(h)


def _split3(x):
    hi = x.astype(BF16)
    r1 = x - hi.astype(F32)
    mid = r1.astype(BF16)
    lo = (r1 - mid.astype(F32)).astype(BF16)
    return jnp.concatenate([hi, mid, lo], axis=0)


def _round_robin(generators):
    alive = list(generators)
    while alive:
        still = []
        for g in alive:
            try:
                next(g)
                still.append(g)
            except StopIteration:
                pass
        alive = still
        yield


def _interleave(generators):
    for _ in _round_robin(generators):
        pass


def _chunk_head_precompute(out, key, c, qh, kh, load_v, bcol, ecol, rcol, diff, incl, strict, eye):
    decay = jnp.exp(diff + incl)
    decay_strict = jnp.exp(diff + strict)
    kb = kh.astype(BF16)
    scores = _dot_nt(jnp.concatenate([kb, qh.astype(BF16)], axis=0), kb)
    yield
    a_mat = scores[:c] * decay_strict * bcol
    attn = (scores[c:] * decay).astype(BF16)
    levels = int(math.log2(c)) - 1
    ab = a_mat.astype(BF16)
    x = _dot(ab, ab)
    yield
    t = eye - a_mat
    for k in range(1, levels + 1):
        xb = x.astype(BF16)
        if k < levels:
            tx = _dot(jnp.concatenate([t.astype(BF16), xb], axis=0), xb)
            yield
            t = t + tx[:c]
            x = tx[c:]
        else:
            tx = _dot(t.astype(BF16), xb)
            yield
            t = t + tx
    rhs = jnp.concatenate([load_v() * bcol, kh * (bcol * ecol)], axis=1)
    uw = _dot(t.astype(BF16), rhs.astype(BF16))
    yield
    out[key] = dict(u=uw[:, :HEAD_DIM], w=uw[:, HEAD_DIM:].astype(BF16),
                    qg=(qh * ecol).astype(BF16), attn=attn, k_rem=(kh * rcol).astype(BF16))


def _chunk_head_recurrence(pre, state, key, decay_last, finish):
    s_old = state[key]
    c = pre["u"].shape[0]
    wq_s = _dot(jnp.concatenate([pre["w"], pre["qg"]], axis=0), s_old.astype(BF16))
    yield
    v_new = (pre["u"] - wq_s[:c]).astype(BF16)
    o = wq_s[c:] + _dot(pre["attn"], v_new)
    state[key] = s_old * decay_last + _dot_tn(pre["k_rem"], v_new)
    yield
    finish(o)


def _normalised_input(x_ref, lnpre_ref):
    bb, tb, _ = x_ref.shape
    x = x_ref[...].reshape(bb * tb, D_MODEL)
    var = jnp.mean(x * x, axis=-1, keepdims=True)
    return (x * lax.rsqrt(var + RMS_EPS) * lnpre_ref[...]).astype(BF16)


def _staged_rows(b, tb, back=0):
    first = b * (PAD + tb) + PAD - back
    return slice(first, first + tb)


def _project_piece(bb, tb, j, hb, win_ref, proj_s, qc_s, zraw_s, gate_s):
    lo = j * PROJ_NBLK
    hi = (j + 1) * PROJ_NBLK
    res = _dot(hb, win_ref[:, lo:hi])
    if lo < D_A:
        proj_s[:, lo:hi] = res
    elif lo < D_A + D_QKV:
        for b in range(bb):
            qc_s[_staged_rows(b, tb), lo - D_A:hi - D_A] = res[b * tb:(b + 1) * tb]
    elif lo < D_MAIN:
        zraw_s[:, lo - D_A - D_QKV:hi - D_A - D_QKV] = res
    else:
        gate_s[...] = res


def _elementwise_pieces(bb, tb, cwa_ref, cwq_ref, alog_ref, dtb_ref, proj_s, ua_s, qc_s, zraw_s,
                        gate_s, qkv_s, zg_s, mix_s, g_s, beta_s):
    def causal_conv(buf_s, w_ref, width, cs):
        rows = []
        for b in range(bb):
            conv = w_ref[0:1, cs] * buf_s[_staged_rows(b, tb, width - 1), cs]
            for k in range(1, width):
                conv = conv + w_ref[k:k + 1, cs] * buf_s[_staged_rows(b, tb, width - 1 - k), cs]
            rows.append(conv)
        return rows[0] if bb == 1 else jnp.concatenate(rows, axis=0)

    def conv_in():
        for j in range(D_CONV // LANES):
            cs = slice(j * LANES, (j + 1) * LANES)
            u = (proj_s[:, D_CONV + j * LANES:D_CONV + (j + 1) * LANES]
                 * proj_s[:, 2 * D_CONV + j * LANES:2 * D_CONV + (j + 1) * LANES])
            for b in range(bb):
                ua_s[_staged_rows(b, tb), cs] = u[b * tb:(b + 1) * tb]

    def group_a(j):
        cs = slice(j * LANES, (j + 1) * LANES)
        conv = causal_conv(ua_s, cwa_ref, CONV_A_WIDTH, cs)
        z = proj_s[:, 3 * D_CONV + j * LANES:3 * D_CONV + (j + 1) * LANES]
        mix_s[:, cs] = (proj_s[:, cs] * conv * _silu(z)).astype(BF16)

    def group_b(j):
        cs = slice(j * LANES, (j + 1) * LANES)
        a = _silu(causal_conv(qc_s, cwq_ref, CONV_QKV_WIDTH, cs))
        if j < 2 * N_HEADS:
            rn = lax.rsqrt(jnp.sum(a * a, axis=-1, keepdims=True) + L2_EPS)
            if j < N_HEADS:
                rn = rn * (HEAD_DIM ** -0.5)
            a = a * rn
        qkv_s[:, cs] = a

    def gates():
        xs = gate_s[:, 0:LANES] + dtb_ref[...]
        softplus = jnp.maximum(xs, 0.0) + jnp.log1p(jnp.exp(-jnp.abs(xs)))
        g_s[...] = -jnp.exp(alog_ref[...]) * softplus
        beta_s[...] = 1.0 / (1.0 + jnp.exp(-gate_s[:, LANES:2 * LANES]))

    def silu_zg():
        zg_s[...] = _silu(zraw_s[...])

    return {"gates": gates, "conv_in": conv_in, "a": group_a, "b": group_b, "silu_zg": silu_zg}


def _layer_kernel(*refs, bb, tb, c, n_t, has_init):
    refs = list(refs)
    x_ref, xnext_ref = refs.pop(0), refs.pop(0)
    if has_init:
        bufa_ref, bufq_ref, s0_ref = refs.pop(0), refs.pop(0), refs.pop(0)
    (lnpre_ref, win_ref, cwa_ref, cwq_ref, alog_ref, dtb_ref, gnw_ref, wout_ref, lnpost_ref,
     y_ref, na_ref, nq_ref, ns_ref,
     h_s, proj_s, ua_s, qc_s, zraw_s, gate_s, qkv_s, zg_s, g_s, beta_s, mix_s) = refs
    nc = tb // c
    s = pl.program_id(0)
    proj_bufs = (win_ref, proj_s, qc_s, zraw_s, gate_s)
    pieces = _elementwise_pieces(bb, tb, cwa_ref, cwq_ref, alog_ref, dtb_ref, proj_s, ua_s, qc_s,
                                 zraw_s, gate_s, qkv_s, zg_s, mix_s, g_s, beta_s)
    tail_rows = lambda b, width: slice(b * (PAD + tb) + PAD - (width - 1), b * (PAD + tb) + PAD)
    last_rows = lambda b, width: slice((b + 1) * (PAD + tb) - (width - 1), (b + 1) * (PAD + tb))

    @pl.when(s == 0)
    def _first_tile_early_pieces():
        hb = _normalised_input(x_ref, lnpre_ref)
        h_s[...] = hb
        for j in EARLY_PIECES:
            _project_piece(bb, tb, j, hb, *proj_bufs)

    @pl.when(s % n_t == 0)
    def _load_stream_state():
        for b in range(bb):
            if has_init:
                ua_s[tail_rows(b, CONV_A_WIDTH), :] = bufa_ref[0, b]
                qc_s[tail_rows(b, CONV_QKV_WIDTH), :] = bufq_ref[0, b]
                ns_ref[0, b] = s0_ref[0, b]
            else:
                ua_s[tail_rows(b, CONV_A_WIDTH), :] = jnp.zeros((CONV_A_WIDTH - 1, D_CONV), F32)
                qc_s[tail_rows(b, CONV_QKV_WIDTH), :] = jnp.zeros((CONV_QKV_WIDTH - 1, D_QKV), F32)
                ns_ref[0, b] = jnp.zeros((N_HEADS, HEAD_DIM, HEAD_DIM), F32)

    hb = h_s[...]
    for kind, *args in EARLY_FOLLOWERS:
        pieces[kind](*args)

    def project_pieces(schedule):
        for j, followers in schedule:
            _project_piece(bb, tb, j, hb, *proj_bufs)
            for kind, *args in followers:
                pieces[kind](*args)
            yield

    _interleave([project_pieces(LATE_FIRST)])

    row = lax.broadcasted_iota(jnp.int32, (c, c), 0)
    col = lax.broadcasted_iota(jnp.int32, (c, c), 1)
    incl = jnp.where(row >= col, 0.0, -jnp.inf)
    strict = jnp.where(row > col, 0.0, -jnp.inf)
    eye = jnp.where(row == col, 1.0, 0.0)
    row3 = lax.broadcasted_iota(jnp.int32, (c, 3 * c), 0)
    col3 = lax.broadcasted_iota(jnp.int32, (c, 3 * c), 1) & (c - 1)
    cumsum_mat = jnp.where(row3 >= col3, 1.0, 0.0).astype(BF16)

    pre = {}
    decay_last = {}
    chunks = [(b, n, slice(b * tb + n * c, b * tb + (n + 1) * c))
              for b in range(bb) for n in range(nc)]
    gate_terms = {}
    for b, n, rs in chunks:
        gb = _dot(cumsum_mat, _split3(g_s[rs, :]))
        gl = gb[c - 1:c, :]
        gb_rows = jnp.concatenate([gb, jnp.zeros((LANES - c, LANES), F32)], axis=0).T
        gate_terms[b, n] = (gb, gb_rows, jnp.exp(gb), jnp.exp(gl - gb), jnp.exp(gl), beta_s[rs, :])

    def head_group_chains(start_round, heads):
        for _ in range(start_round):
            yield
        chains = []
        for b, n, rs in chunks:
            gb, gb_rows, e_gb, e_rem, e_gl, beta_c = gate_terms[b, n]
            for hd in heads:
                qh = qkv_s[rs, hd * HEAD_DIM:(hd + 1) * HEAD_DIM]
                kh = qkv_s[rs, D_GDN + hd * HEAD_DIM:D_GDN + (hd + 1) * HEAD_DIM]
                load_v = functools.partial(
                    lambda rows, h: qkv_s[rows, 2 * D_GDN + h * HEAD_DIM:2 * D_GDN + (h + 1) * HEAD_DIM],
                    rs, hd)
                diff = gb[:, hd:hd + 1] - gb_rows[hd:hd + 1, 0:c]
                decay_last[b, n, hd] = e_gl[:, hd:hd + 1]
                chains.append(_chunk_head_precompute(
                    pre, (b, n, hd), c, qh, kh, load_v, beta_c[:, hd:hd + 1], e_gb[:, hd:hd + 1],
                    e_rem[:, hd:hd + 1], diff, incl, strict, eye))
        yield from _round_robin(chains)

    _interleave([project_pieces(LATE_REST)]
                + [head_group_chains(start, heads) for start, heads in HEAD_GROUPS])
    for b in range(bb):
        tail_a = ua_s[last_rows(b, CONV_A_WIDTH), :]
        ua_s[tail_rows(b, CONV_A_WIDTH), :] = tail_a
        na_ref[0, b] = tail_a
        tail_q = qc_s[last_rows(b, CONV_QKV_WIDTH), :]
        qc_s[tail_rows(b, CONV_QKV_WIDTH), :] = tail_q
        nq_ref[0, b] = tail_q

    def finish_head(rs, hd):
        def finish(o):
            o_var = jnp.mean(o * o, axis=-1, keepdims=True)
            o = o * lax.rsqrt(o_var + RMS_EPS) * gnw_ref[...]
            mix_s[rs, D_CONV + hd * HEAD_DIM:D_CONV + (hd + 1) * HEAD_DIM] = (
                o * zg_s[rs, hd * HEAD_DIM:(hd + 1) * HEAD_DIM]).astype(BF16)
        return finish

    state = {(b, hd): ns_ref[0, b, hd] for b in range(bb) for hd in range(N_HEADS)}

    def recurrence():
        for n in range(nc):
            yield from _round_robin([
                _chunk_head_recurrence(pre[b, n, hd], state, (b, hd), decay_last[b, n, hd],
                                       finish_head(slice(b * tb + n * c, b * tb + (n + 1) * c), hd))
                for b in range(bb) for hd in range(N_HEADS)])

    def next_tile_early_pieces():
        hb_next = _normalised_input(xnext_ref, lnpre_ref)
        h_s[...] = hb_next
        for j in EARLY_PIECES:
            _project_piece(bb, tb, j, hb_next, *proj_bufs)
            yield

    _interleave([recurrence(), next_tile_early_pieces()])
    for (b, hd), value in state.items():
        ns_ref[0, b, hd] = value

    out = _dot(mix_s[...], wout_ref[...])
    out_var = jnp.mean(out * out, axis=-1, keepdims=True)
    y = x_ref[...].reshape(bb * tb, D_MODEL) + out * lax.rsqrt(out_var + RMS_EPS) * lnpost_ref[...]
    y_ref[...] = y.reshape(bb, tb, D_MODEL)


def _layer_call(x, init, params, *, bb, tb, c, name):
    bsz, seq, _ = x.shape
    assert bsz % bb == 0 and seq % tb == 0 and tb % c == 0 and tb % SUBLANES == 0 and c <= LANES
    has_init = init is not None
    n_t = seq // tb
    n_tiles = (bsz // bb) * n_t
    rows = bb * tb

    def this_tile(s):
        return s

    def next_tile(s):
        return jnp.minimum(s + 1, n_tiles - 1)

    tile_spec = lambda tile: pl.BlockSpec((bb, tb, D_MODEL), lambda s: (tile(s) // n_t, tile(s) % n_t, 0))
    stream_specs = [
        pl.BlockSpec((1, bb, CONV_A_WIDTH - 1, D_CONV), lambda s: (0, s // n_t, 0, 0)),
        pl.BlockSpec((1, bb, CONV_QKV_WIDTH - 1, D_QKV), lambda s: (0, s // n_t, 0, 0)),
        pl.BlockSpec((1, bb, N_HEADS, HEAD_DIM, HEAD_DIM), lambda s: (0, s // n_t, 0, 0, 0)),
    ]
    in_specs = [tile_spec(this_tile), tile_spec(next_tile)]
    operands = [x, x]
    if has_init:
        in_specs += stream_specs
        operands += list(init)
    in_specs += [pl.BlockSpec(p.shape, lambda s, nd=p.ndim: (0,) * nd) for p in params]
    operands += list(params)
    out_shape = (
        jax.ShapeDtypeStruct(x.shape, x.dtype),
        jax.ShapeDtypeStruct((1, bsz, CONV_A_WIDTH - 1, D_CONV), F32),
        jax.ShapeDtypeStruct((1, bsz, CONV_QKV_WIDTH - 1, D_QKV), F32),
        jax.ShapeDtypeStruct((1, bsz, N_HEADS, HEAD_DIM, HEAD_DIM), F32),
    )
    out_specs = [tile_spec(this_tile)] + stream_specs
    scratch = [
        pltpu.VMEM((rows, D_MODEL), BF16),
        pltpu.VMEM((rows, D_A), F32),
        pltpu.VMEM((bb * (PAD + tb), D_CONV), F32),
        pltpu.VMEM((bb * (PAD + tb), D_QKV), F32),
        pltpu.VMEM((rows, D_GDN), F32),
        pltpu.VMEM((rows, GATE_W), F32),
        pltpu.VMEM((rows, D_QKV), F32),
        pltpu.VMEM((rows, D_GDN), F32),
        pltpu.VMEM((rows, LANES), F32),
        pltpu.VMEM((rows, LANES), F32),
        pltpu.VMEM((rows, D_MIX), BF16),
    ]
    return pl.pallas_call(
        functools.partial(_layer_kernel, bb=bb, tb=tb, c=c, n_t=n_t, has_init=has_init),
        grid=(n_tiles,),
        in_specs=in_specs,
        out_specs=out_specs,
        out_shape=out_shape,
        scratch_shapes=scratch,
        compiler_params=pltpu.CompilerParams(
            dimension_semantics=("arbitrary",),
            vmem_limit_bytes=VMEM_LIMIT_BYTES),
        name=name,
    )(*operands)


def _tile_shape(bsz, seq):
    tb = min(seq, TILE_ROWS)
    bb = max(1, math.gcd(bsz, TILE_ROWS // tb))
    return bb, tb


def _prepare_params(ln_pre, w_in, conv_a_w, conv_qkv_w, a_log, dt_bias, gdn_norm_w, w_out, ln_post):
    w = w_in.astype(BF16)
    gap = jnp.zeros((D_MODEL, LANES - N_HEADS), BF16)
    w_all = jnp.concatenate([w[:, :D_MAIN + N_HEADS], gap, w[:, D_MAIN + N_HEADS:], gap], axis=1)
    pad_heads = lambda v: jnp.pad(v, (0, LANES - N_HEADS)).reshape(1, LANES)
    return (ln_pre.reshape(1, D_MODEL), w_all, conv_a_w, conv_qkv_w,
            pad_heads(a_log), pad_heads(dt_bias), gdn_norm_w.reshape(1, HEAD_DIM),
            w_out.astype(BF16), ln_post.reshape(1, D_MODEL))


def kernel(x_prompt, x_sample, cache_conv_a, cache_conv_qkv, state_gdn, ln_pre, w_in, conv_a_w,
           conv_qkv_w, a_log, dt_bias, gdn_norm_w, w_out, ln_post):
    assert ln_pre.shape[0] == 1, "single-layer kernel"
    params = _prepare_params(ln_pre[0], w_in[0], conv_a_w[0], conv_qkv_w[0], a_log[0], dt_bias[0],
                             gdn_norm_w[0], w_out[0], ln_post[0])
    bb_p, tb_p = _tile_shape(*x_prompt.shape[:2])
    bb_s, tb_s = _tile_shape(*x_sample.shape[:2])
    y_p, a_p, q_p, s_p = _layer_call(x_prompt, None, params, bb=bb_p, tb=tb_p,
                                     c=math.gcd(x_prompt.shape[1], CHUNK), name="hybrid_prompt")
    y_s, a_s, q_s, s_s = _layer_call(x_sample, (cache_conv_a, cache_conv_qkv, state_gdn), params,
                                     bb=bb_s, tb=tb_s, c=math.gcd(x_sample.shape[1], CHUNK),
                                     name="hybrid_sample")
    return (y_p, y_s, a_p, q_p, s_p, a_s, q_s, s_s)
```

```python
import functools
import math

import jax
import jax.numpy as jnp
from jax import lax
from jax.experimental import pallas as pl
from jax.experimental.pallas import tpu as pltpu

D_MODEL = 1024
D_CONV = 512
CONV_A_WIDTH = 3
N_HEADS = 4
HEAD_DIM = 128
D_GDN = N_HEADS * HEAD_DIM
D_QKV = 3 * D_GDN
CONV_QKV_WIDTH = 4
D_MIX = D_CONV + D_GDN
D_A = 4 * D_CONV
D_MAIN = D_A + D_QKV + D_GDN
CHUNK = 64
RMS_EPS = 1e-6
L2_EPS = 1e-6

LANES = 128
SUBLANES = 8
PAD = SUBLANES
GATE_W = 2 * LANES
PROJ_NBLK = 256
TILE_ROWS = 256
VMEM_LIMIT_BYTES = 52 * 1024 * 1024

EARLY_PIECES = (16, 2, 3, 4, 5, 0, 6, 1, 7)
EARLY_FOLLOWERS = (("gates",), ("conv_in",), ("a", 0), ("a", 1), ("a", 2), ("a", 3))
_QKV_PIECE = lambda m: (D_A // PROJ_NBLK + m, (("b", 2 * m), ("b", 2 * m + 1)))
LATE_FIRST = (_QKV_PIECE(0), _QKV_PIECE(2))
LATE_REST = (_QKV_PIECE(1), _QKV_PIECE(3), _QKV_PIECE(4), _QKV_PIECE(5),
             (14, ()), (15, (("silu_zg",),)))
HEAD_GROUPS = ((0, (0, 1)), (2, (2, 3)))
OUT_PROJ_ROUNDS = (0, 0, 1, 1)

F32 = jnp.float32
BF16 = jnp.bfloat16


def _dot(a, b):
    return jnp.dot(a, b, preferred_element_type=F32)


def _dot_nt(a, b):
    return lax.dot_general(a, b, (((1,), (1,)), ((), ())), preferred_element_type=F32)


def _dot_tn(a, b):
    return lax.dot_general(a, b, (((0,), (0,)), ((), ())), preferred_element_type=F32)


def _silu(z):
    h = 0.5 * z
    return h + h * jnp.tanh(h)


def _split3(x):
    hi = x.astype(BF16)
    r1 = x - hi.astype(F32)
    mid = r1.astype(BF16)
    lo = (r1 - mid.astype(F32)).astype(BF16)
    return jnp.concatenate([hi, mid, lo], axis=0)


def _round_robin(generators):
    alive = list(generators)
    while alive:
        still = []
        for g in alive:
            try:
                next(g)
                still.append(g)
            except StopIteration:
                pass
        alive = still
        yield


def _interleave(generators):
    for _ in _round_robin(generators):
        pass


def _chunk_head_precompute(out, key, c, qh, kh, load_v, bcol, ecol, rcol, diff, incl, strict, eye):
    decay = jnp.exp(diff + incl)
    decay_strict = jnp.exp(diff + strict)
    kb = kh.astype(BF16)
    scores = _dot_nt(jnp.concatenate([kb, qh.astype(BF16)], axis=0), kb)
    yield
    a_mat = scores[:c] * decay_strict * bcol
    attn = (scores[c:] * decay).astype(BF16)
    levels = int(math.log2(c)) - 1
    ab = a_mat.astype(BF16)
    x = _dot(ab, ab)
    yield
    t = eye - a_mat
    for k in range(1, levels + 1):
        xb = x.astype(BF16)
        if k < levels:
            tx = _dot(jnp.concatenate([t.astype(BF16), xb], axis=0), xb)
            yield
            t = t + tx[:c]
            x = tx[c:]
        else:
            tx = _dot(t.astype(BF16), xb)
            yield
            t = t + tx
    rhs = jnp.concatenate([load_v() * bcol, kh * (bcol * ecol)], axis=1)
    uw = _dot(t.astype(BF16), rhs.astype(BF16))
    yield
    out[key] = dict(u=uw[:, :HEAD_DIM], w=uw[:, HEAD_DIM:].astype(BF16),
                    qg=(qh * ecol).astype(BF16), attn=attn, k_rem=(kh * rcol).astype(BF16))


def _chunk_head_recurrence(pre, state, key, decay_last, finish):
    s_old = state[key]
    c = pre["u"].shape[0]
    wq_s = _dot(jnp.concatenate([pre["w"], pre["qg"]], axis=0), s_old.astype(BF16))
    yield
    v_new = (pre["u"] - wq_s[:c]).astype(BF16)
    o = wq_s[c:] + _dot(pre["attn"], v_new)
    state[key] = s_old * decay_last + _dot_tn(pre["k_rem"], v_new)
    yield
    finish(o)


def _normalised_input(x_ref, lnpre_ref):
    bb, tb, _ = x_ref.shape
    x = x_ref[...].reshape(bb * tb, D_MODEL)
    var = jnp.mean(x * x, axis=-1, keepdims=True)
    return (x * lax.rsqrt(var + RMS_EPS) * lnpre_ref[...]).astype(BF16)


def _staged_rows(b, tb, back=0):
    first = b * (PAD + tb) + PAD - back
    return slice(first, first + tb)


def _project_piece(bb, tb, j, hb, win_ref, proj_s, qc_s, zraw_s, gate_s):
    lo = j * PROJ_NBLK
    hi = (j + 1) * PROJ_NBLK
    res = _dot(hb, win_ref[:, lo:hi])
    if lo < D_A:
        proj_s[:, lo:hi] = res
    elif lo < D_A + D_QKV:
        for b in range(bb):
            qc_s[_staged_rows(b, tb), lo - D_A:hi - D_A] = res[b * tb:(b + 1) * tb]
    elif lo < D_MAIN:
        zraw_s[:, lo - D_A - D_QKV:hi - D_A - D_QKV] = res
    else:
        gate_s[...] = res


def _elementwise_pieces(bb, tb, cwa_ref, cwq_ref, alog_ref, dtb_ref, proj_s, ua_s, qc_s, zraw_s,
                        gate_s, qkv_s, zg_s, ya_s, g_s, beta_s):
    def causal_conv(buf_s, w_ref, width, cs):
        rows = []
        for b in range(bb):
            conv = w_ref[0:1, cs] * buf_s[_staged_rows(b, tb, width - 1), cs]
            for k in range(1, width):
                conv = conv + w_ref[k:k + 1, cs] * buf_s[_staged_rows(b, tb, width - 1 - k), cs]
            rows.append(conv)
        return rows[0] if bb == 1 else jnp.concatenate(rows, axis=0)

    def conv_in():
        for j in range(D_CONV // LANES):
            cs = slice(j * LANES, (j + 1) * LANES)
            u = (proj_s[:, D_CONV + j * LANES:D_CONV + (j + 1) * LANES]
                 * proj_s[:, 2 * D_CONV + j * LANES:2 * D_CONV + (j + 1) * LANES])
            for b in range(bb):
                ua_s[_staged_rows(b, tb), cs] = u[b * tb:(b + 1) * tb]

    def group_a(j):
        cs = slice(j * LANES, (j + 1) * LANES)
        conv = causal_conv(ua_s, cwa_ref, CONV_A_WIDTH, cs)
        z = proj_s[:, 3 * D_CONV + j * LANES:3 * D_CONV + (j + 1) * LANES]
        ya_s[:, cs] = (proj_s[:, cs] * conv * _silu(z)).astype(BF16)

    def group_b(j):
        cs = slice(j * LANES, (j + 1) * LANES)
        a = _silu(causal_conv(qc_s, cwq_ref, CONV_QKV_WIDTH, cs))
        if j < 2 * N_HEADS:
            rn = lax.rsqrt(jnp.sum(a * a, axis=-1, keepdims=True) + L2_EPS)
            if j < N_HEADS:
                rn = rn * (HEAD_DIM ** -0.5)
            a = a * rn
        qkv_s[:, cs] = a

    def gates():
        xs = gate_s[:, 0:LANES] + dtb_ref[...]
        softplus = jnp.maximum(xs, 0.0) + jnp.log1p(jnp.exp(-jnp.abs(xs)))
        g_s[...] = -jnp.exp(alog_ref[...]) * softplus
        beta_s[...] = 1.0 / (1.0 + jnp.exp(-gate_s[:, LANES:2 * LANES]))

    def silu_zg():
        zg_s[...] = _silu(zraw_s[...])

    return {"gates": gates, "conv_in": conv_in, "a": group_a, "b": group_b, "silu_zg": silu_zg}


def _layer_kernel(*refs, bb, tb, c, n_t, n_tiles, has_init):
    refs = list(refs)
    xprev_ref, xnext_ref = refs.pop(0), refs.pop(0)
    if has_init:
        bufa_ref, bufq_ref, s0_ref = refs.pop(0), refs.pop(0), refs.pop(0)
    (lnpre_ref, win_ref, cwa_ref, cwq_ref, alog_ref, dtb_ref, gnw_ref, wout_ref, lnpost_ref,
     y_ref, na_ref, nq_ref, ns_ref,
     h_s, proj_s, ua_s, qc_s, zraw_s, gate_s, qkv_s, zg_s, g_s, beta_s, ya_s, mix_s, out_s) = refs
    nc = tb // c
    s = pl.program_id(0)
    proj_bufs = (win_ref, proj_s, qc_s, zraw_s, gate_s)
    pieces = _elementwise_pieces(bb, tb, cwa_ref, cwq_ref, alog_ref, dtb_ref, proj_s, ua_s, qc_s,
                                 zraw_s, gate_s, qkv_s, zg_s, ya_s, g_s, beta_s)
    tail_rows = lambda b, width: slice(b * (PAD + tb) + PAD - (width - 1), b * (PAD + tb) + PAD)
    last_rows = lambda b, width: slice((b + 1) * (PAD + tb) - (width - 1), (b + 1) * (PAD + tb))

    @pl.when(s == 0)
    def _first_tile_early_pieces():
        hb = _normalised_input(xprev_ref, lnpre_ref)
        h_s[...] = hb
        for j in EARLY_PIECES:
            _project_piece(bb, tb, j, hb, *proj_bufs)
        mix_s[...] = jnp.zeros(mix_s.shape, BF16)

    @pl.when((s % n_t == 0) & (s < n_tiles))
    def _load_stream_state():
        for b in range(bb):
            if has_init:
                ua_s[tail_rows(b, CONV_A_WIDTH), :] = bufa_ref[0, b]
                qc_s[tail_rows(b, CONV_QKV_WIDTH), :] = bufq_ref[0, b]
                ns_ref[0, b] = s0_ref[0, b]
            else:
                ua_s[tail_rows(b, CONV_A_WIDTH), :] = jnp.zeros((CONV_A_WIDTH - 1, D_CONV), F32)
                qc_s[tail_rows(b, CONV_QKV_WIDTH), :] = jnp.zeros((CONV_QKV_WIDTH - 1, D_QKV), F32)
                ns_ref[0, b] = jnp.zeros((N_HEADS, HEAD_DIM, HEAD_DIM), F32)

    hb = h_s[...]
    for kind, *args in EARLY_FOLLOWERS:
        pieces[kind](*args)

    def project_pieces(schedule):
        for j, followers in schedule:
            _project_piece(bb, tb, j, hb, *proj_bufs)
            for kind, *args in followers:
                pieces[kind](*args)
            yield

    _interleave([project_pieces(LATE_FIRST)])

    row = lax.broadcasted_iota(jnp.int32, (c, c), 0)
    col = lax.broadcasted_iota(jnp.int32, (c, c), 1)
    incl = jnp.where(row >= col, 0.0, -jnp.inf)
    strict = jnp.where(row > col, 0.0, -jnp.inf)
    eye = jnp.where(row == col, 1.0, 0.0)
    row3 = lax.broadcasted_iota(jnp.int32, (c, 3 * c), 0)
    col3 = lax.broadcasted_iota(jnp.int32, (c, 3 * c), 1) & (c - 1)
    cumsum_mat = jnp.where(row3 >= col3, 1.0, 0.0).astype(BF16)

    pre = {}
    decay_last = {}
    chunks = [(b, n, slice(b * tb + n * c, b * tb + (n + 1) * c))
              for b in range(bb) for n in range(nc)]
    gate_terms = {}
    for b, n, rs in chunks:
        gb = _dot(cumsum_mat, _split3(g_s[rs, :]))
        gl = gb[c - 1:c, :]
        gb_rows = jnp.concatenate([gb, jnp.zeros((LANES - c, LANES), F32)], axis=0).T
        gate_terms[b, n] = (gb, gb_rows, jnp.exp(gb), jnp.exp(gl - gb), jnp.exp(gl), beta_s[rs, :])

    def head_group_chains(start_round, heads):
        for _ in range(start_round):
            yield
        chains = []
        for b, n, rs in chunks:
            gb, gb_rows, e_gb, e_rem, e_gl, beta_c = gate_terms[b, n]
            for hd in heads:
                qh = qkv_s[rs, hd * HEAD_DIM:(hd + 1) * HEAD_DIM]
                kh = qkv_s[rs, D_GDN + hd * HEAD_DIM:D_GDN + (hd + 1) * HEAD_DIM]
                load_v = functools.partial(
                    lambda rows, h: qkv_s[rows, 2 * D_GDN + h * HEAD_DIM:2 * D_GDN + (h + 1) * HEAD_DIM],
                    rs, hd)
                diff = gb[:, hd:hd + 1] - gb_rows[hd:hd + 1, 0:c]
                decay_last[b, n, hd] = e_gl[:, hd:hd + 1]
                chains.append(_chunk_head_precompute(
                    pre, (b, n, hd), c, qh, kh, load_v, beta_c[:, hd:hd + 1], e_gb[:, hd:hd + 1],
                    e_rem[:, hd:hd + 1], diff, incl, strict, eye))
        yield from _round_robin(chains)

    def previous_tile_output():
        mix = mix_s[...]
        this_round = 0
        for j, at_round in enumerate(OUT_PROJ_ROUNDS):
            while this_round < at_round:
                yield
                this_round += 1
            cols = slice(j * PROJ_NBLK, (j + 1) * PROJ_NBLK)
            out_s[:, cols] = _dot(mix, wout_ref[:, cols])
        yield
        out = out_s[...]
        out_var = jnp.mean(out * out, axis=-1, keepdims=True)
        y = (xprev_ref[...].reshape(bb * tb, D_MODEL)
             + out * lax.rsqrt(out_var + RMS_EPS) * lnpost_ref[...])
        y_ref[...] = y.reshape(bb, tb, D_MODEL)

    _interleave([project_pieces(LATE_REST), previous_tile_output()]
                + [head_group_chains(start, heads) for start, heads in HEAD_GROUPS])

    def finish_head(rs, hd):
        def finish(o):
            o_var = jnp.mean(o * o, axis=-1, keepdims=True)
            o = o * lax.rsqrt(o_var + RMS_EPS) * gnw_ref[...]
            mix_s[rs, D_CONV + hd * HEAD_DIM:D_CONV + (hd + 1) * HEAD_DIM] = (
                o * zg_s[rs, hd * HEAD_DIM:(hd + 1) * HEAD_DIM]).astype(BF16)
        return finish

    state = {(b, hd): ns_ref[0, b, hd] for b in range(bb) for hd in range(N_HEADS)}
    mix_s[:, 0:D_CONV] = ya_s[...]

    def recurrence():
        for n in range(nc):
            yield from _round_robin([
                _chunk_head_recurrence(pre[b, n, hd], state, (b, hd), decay_last[b, n, hd],
                                       finish_head(slice(b * tb + n * c, b * tb + (n + 1) * c), hd))
                for b in range(bb) for hd in range(N_HEADS)])

    def next_tile_early_pieces():
        hb_next = _normalised_input(xnext_ref, lnpre_ref)
        h_s[...] = hb_next
        for j in EARLY_PIECES:
            _project_piece(bb, tb, j, hb_next, *proj_bufs)
            yield

    _interleave([recurrence(), next_tile_early_pieces()])

    @pl.when(s < n_tiles)
    def _carry_stream_state():
        for (b, hd), value in state.items():
            ns_ref[0, b, hd] = value
        for b in range(bb):
            tail_a = ua_s[last_rows(b, CONV_A_WIDTH), :]
            ua_s[tail_rows(b, CONV_A_WIDTH), :] = tail_a
            na_ref[0, b] = tail_a
            tail_q = qc_s[last_rows(b, CONV_QKV_WIDTH), :]
            qc_s[tail_rows(b, CONV_QKV_WIDTH), :] = tail_q
            nq_ref[0, b] = tail_q


def _layer_call(x, init, params, *, bb, tb, c, name):
    bsz, seq, _ = x.shape
    assert bsz % bb == 0 and seq % tb == 0 and tb % c == 0 and tb % SUBLANES == 0 and c <= LANES
    has_init = init is not None
    n_t = seq // tb
    n_tiles = (bsz // bb) * n_t
    rows = bb * tb

    def this_tile(s):
        return jnp.minimum(s, n_tiles - 1)

    def prev_tile(s):
        return jnp.maximum(s - 1, 0)

    def next_tile(s):
        return jnp.minimum(s + 1, n_tiles - 1)

    tile_spec = lambda tile: pl.BlockSpec((bb, tb, D_MODEL), lambda s: (tile(s) // n_t, tile(s) % n_t, 0))
    stream_specs = [
        pl.BlockSpec((1, bb, CONV_A_WIDTH - 1, D_CONV), lambda s: (0, this_tile(s) // n_t, 0, 0)),
        pl.BlockSpec((1, bb, CONV_QKV_WIDTH - 1, D_QKV), lambda s: (0, this_tile(s) // n_t, 0, 0)),
        pl.BlockSpec((1, bb, N_HEADS, HEAD_DIM, HEAD_DIM),
                     lambda s: (0, this_tile(s) // n_t, 0, 0, 0)),
    ]
    in_specs = [tile_spec(prev_tile), tile_spec(next_tile)]
    operands = [x, x]
    if has_init:
        in_specs += stream_specs
        operands += list(init)
    in_specs += [pl.BlockSpec(p.shape, lambda s, nd=p.ndim: (0,) * nd) for p in params]
    operands += list(params)
    out_shape = (
        jax.ShapeDtypeStruct(x.shape, x.dtype),
        jax.ShapeDtypeStruct((1, bsz, CONV_A_WIDTH - 1, D_CONV), F32),
        jax.ShapeDtypeStruct((1, bsz, CONV_QKV_WIDTH - 1, D_QKV), F32),
        jax.ShapeDtypeStruct((1, bsz, N_HEADS, HEAD_DIM, HEAD_DIM), F32),
    )
    out_specs = [tile_spec(prev_tile)] + stream_specs
    scratch = [
        pltpu.VMEM((rows, D_MODEL), BF16),
        pltpu.VMEM((rows, D_A), F32),
        pltpu.VMEM((bb * (PAD + tb), D_CONV), F32),
        pltpu.VMEM((bb * (PAD + tb), D_QKV), F32),
        pltpu.VMEM((rows, D_GDN), F32),
        pltpu.VMEM((rows, GATE_W), F32),
        pltpu.VMEM((rows, D_QKV), F32),
        pltpu.VMEM((rows, D_GDN), F32),
        pltpu.VMEM((rows, LANES), F32),
        pltpu.VMEM((rows, LANES), F32),
        pltpu.VMEM((rows, D_CONV), BF16),
        pltpu.VMEM((rows, D_MIX), BF16),
        pltpu.VMEM((rows, D_MODEL), F32),
    ]
    return pl.pallas_call(
        functools.partial(_layer_kernel, bb=bb, tb=tb, c=c, n_t=n_t, n_tiles=n_tiles,
                          has_init=has_init),
        grid=(n_tiles + 1,),
        in_specs=in_specs,
        out_specs=out_specs,
        out_shape=out_shape,
        scratch_shapes=scratch,
        compiler_params=pltpu.CompilerParams(
            dimension_semantics=("arbitrary",),
            vmem_limit_bytes=VMEM_LIMIT_BYTES),
        name=name,
    )(*operands)


def _tile_shape(bsz, seq):
    tb = min(seq, TILE_ROWS)
    bb = max(1, math.gcd(bsz, TILE_ROWS // tb))
    return bb, tb


def _prepare_params(ln_pre, w_in, conv_a_w, conv_qkv_w, a_log, dt_bias, gdn_norm_w, w_out, ln_post):
    w = w_in.astype(BF16)
    gap = jnp.zeros((D_MODEL, LANES - N_HEADS), BF16)
    w_all = jnp.concatenate([w[:, :D_MAIN + N_HEADS], gap, w[:, D_MAIN + N_HEADS:], gap], axis=1)
    pad_heads = lambda v: jnp.pad(v, (0, LANES - N_HEADS)).reshape(1, LANES)
    return (ln_pre.reshape(1, D_MODEL), w_all, conv_a_w, conv_qkv_w,
            pad_heads(a_log), pad_heads(dt_bias), gdn_norm_w.reshape(1, HEAD_DIM),
            w_out.astype(BF16), ln_post.reshape(1, D_MODEL))


def kernel(x_prompt, x_sample, cache_conv_a, cache_conv_qkv, state_gdn, ln_pre, w_in, conv_a_w,
           conv_qkv_w, a_log, dt_bias, gdn_norm_w, w_out, ln_post):
    assert ln_pre.shape[0] == 1, "single-layer kernel"
    params = _prepare_params(ln_pre[0], w_in[0], conv_a_w[0], conv_qkv_w[0], a_log[0], dt_bias[0],
                             gdn_norm_w[0], w_out[0], ln_post[0])
    bb_p, tb_p = _tile_shape(*x_prompt.shape[:2])
    bb_s, tb_s = _tile_shape(*x_sample.shape[:2])
    y_p, a_p, q_p, s_p = _layer_call(x_prompt, None, params, bb=bb_p, tb=tb_p,
                                     c=math.gcd(x_prompt.shape[1], CHUNK), name="hybrid_prompt")
    y_s, a_s, q_s, s_s = _layer_call(x_sample, (cache_conv_a, cache_conv_qkv, state_gdn), params,
                                     bb=bb_s, tb=tb_s, c=math.gcd(x_sample.shape[1], CHUNK),
                                     name="hybrid_sample")
    return (y_p, y_s, a_p, q_p, s_p, a_s, q_s, s_s)
```

```python
import functools
import math

import jax
import jax.numpy as jnp
from jax import lax
from jax.experimental import pallas as pl
from jax.experimental.pallas import tpu as pltpu

D_MODEL = 1024
D_CONV = 512
CONV_A_WIDTH = 3
N_HEADS = 4
HEAD_DIM = 128
D_GDN = N_HEADS * HEAD_DIM
D_QKV = 3 * D_GDN
CONV_QKV_WIDTH = 4
D_MIX = D_CONV + D_GDN
D_A = 4 * D_CONV
D_MAIN = D_A + D_QKV + D_GDN
CHUNK = 64
RMS_EPS = 1e-6
L2_EPS = 1e-6

LANES = 128
SUBLANES = 8
PAD = SUBLANES
GATE_W = 2 * LANES
PROJ_NBLK = 256
TILE_ROWS = 256
VMEM_LIMIT_BYTES = 56 * 1024 * 1024

EARLY_PIECES = (16, 2, 3, 4, 5, 0, 6, 1, 7)
EARLY_FOLLOWERS = (("gates",), ("conv_in",), ("a", 0), ("a", 1), ("a", 2), ("a", 3))
_QKV_PIECE = lambda m: (D_A // PROJ_NBLK + m, (("b", 2 * m), ("b", 2 * m + 1)))
LATE_FIRST = (_QKV_PIECE(0), _QKV_PIECE(2))
LATE_REST = (_QKV_PIECE(1), _QKV_PIECE(3), _QKV_PIECE(4), _QKV_PIECE(5),
             (14, ()), (15, (("silu_zg",),)))
HEAD_GROUPS = ((0, (0, 1)), (2, (2, 3)))
OUT_PROJ_ROUNDS = (0, 0, 1, 1)

F32 = jnp.float32
BF16 = jnp.bfloat16
SINGLE_BUFFERED_PARAMS = (1, 7)


def _dot(a, b):
    return jnp.dot(a, b, preferred_element_type=F32)


def _dot_nt(a, b):
    return lax.dot_general(a, b, (((1,), (1,)), ((), ())), preferred_element_type=F32)


def _dot_tn(a, b):
    return lax.dot_general(a, b, (((0,), (0,)), ((), ())), preferred_element_type=F32)


def _silu(z):
    h = 0.5 * z
    return h + h * jnp.tanh(h)


def _split3(x):
    hi = x.astype(BF16)
    r1 = x - hi.astype(F32)
    mid = r1.astype(BF16)
    lo = (r1 - mid.astype(F32)).astype(BF16)
    return jnp.concatenate([hi, mid, lo], axis=0)


def _round_robin(generators):
    alive = list(generators)
    while alive:
        still = []
        for g in alive:
            try:
                next(g)
                still.append(g)
            except StopIteration:
                pass
        alive = still
        yield


def _interleave(generators):
    for _ in _round_robin(generators):
        pass


def _chunk_head_precompute(out, key, c, qh, kh, load_v, bcol, ecol, rcol, diff, incl, strict, eye):
    decay = jnp.exp(diff + incl)
    decay_strict = jnp.exp(diff + strict)
    kb = kh.astype(BF16)
    scores = _dot_nt(jnp.concatenate([kb, qh.astype(BF16)], axis=0), kb)
    yield
    a_mat = scores[:c] * decay_strict * bcol
    attn = (scores[c:] * decay).astype(BF16)
    levels = int(math.log2(c)) - 1
    ab = a_mat.astype(BF16)
    x = _dot(ab, ab)
    yield
    t = eye - a_mat
    for k in range(1, levels + 1):
        xb = x.astype(BF16)
        if k < levels:
            tx = _dot(jnp.concatenate([t.astype(BF16), xb], axis=0), xb)
            yield
            t = t + tx[:c]
            x = tx[c:]
        else:
            tx = _dot(t.astype(BF16), xb)
            yield
            t = t + tx
    rhs = jnp.concatenate([load_v() * bcol, kh * (bcol * ecol)], axis=1)
    uw = _dot(t.astype(BF16), rhs.astype(BF16))
    yield
    out[key] = dict(u=uw[:, :HEAD_DIM], w=uw[:, HEAD_DIM:].astype(BF16),
                    qg=(qh * ecol).astype(BF16), attn=attn, k_rem=(kh * rcol).astype(BF16))


def _chunk_head_recurrence(pre, state, key, decay_last, finish):
    s_old = state[key]
    c = pre["u"].shape[0]
    wq_s = _dot(jnp.concatenate([pre["w"], pre["qg"]], axis=0), s_old.astype(BF16))
    yield
    v_new = (pre["u"] - wq_s[:c]).astype(BF16)
    o = wq_s[c:] + _dot(pre["attn"], v_new)
    state[key] = s_old * decay_last + _dot_tn(pre["k_rem"], v_new)
    yield
    finish(o)


def _normalised_input(x_ref, lnpre_ref):
    bb, tb, _ = x_ref.shape
    x = x_ref[...].reshape(bb * tb, D_MODEL)
    var = jnp.mean(x * x, axis=-1, keepdims=True)
    return (x * lax.rsqrt(var + RMS_EPS) * lnpre_ref[...]).astype(BF16)


def _staged_rows(b, tb, back=0):
    first = b * (PAD + tb) + PAD - back
    return slice(first, first + tb)


def _project_piece(bb, tb, j, hb, win_ref, proj_s, qc_s, zraw_s, gate_s):
    lo = j * PROJ_NBLK
    hi = (j + 1) * PROJ_NBLK
    res = _dot(hb, win_ref[:, lo:hi])
    if lo < D_A:
        proj_s[:, lo:hi] = res
    elif lo < D_A + D_QKV:
        for b in range(bb):
            qc_s[_staged_rows(b, tb), lo - D_A:hi - D_A] = res[b * tb:(b + 1) * tb]
    elif lo < D_MAIN:
        zraw_s[:, lo - D_A - D_QKV:hi - D_A - D_QKV] = res
    else:
        gate_s[...] = res


def _elementwise_pieces(bb, tb, cwa_ref, cwq_ref, alog_ref, dtb_ref, proj_s, ua_s, qc_s, zraw_s,
                        gate_s, qkv_s, zg_s, ya_s, g_s, beta_s):
    def causal_conv(buf_s, w_ref, width, cs):
        rows = []
        for b in range(bb):
            conv = w_ref[0:1, cs] * buf_s[_staged_rows(b, tb, width - 1), cs]
            for k in range(1, width):
                conv = conv + w_ref[k:k + 1, cs] * buf_s[_staged_rows(b, tb, width - 1 - k), cs]
            rows.append(conv)
        return rows[0] if bb == 1 else jnp.concatenate(rows, axis=0)

    def conv_in():
        for j in range(D_CONV // LANES):
            cs = slice(j * LANES, (j + 1) * LANES)
            u = (proj_s[:, D_CONV + j * LANES:D_CONV + (j + 1) * LANES]
                 * proj_s[:, 2 * D_CONV + j * LANES:2 * D_CONV + (j + 1) * LANES])
            for b in range(bb):
                ua_s[_staged_rows(b, tb), cs] = u[b * tb:(b + 1) * tb]

    def group_a(j):
        cs = slice(j * LANES, (j + 1) * LANES)
        conv = causal_conv(ua_s, cwa_ref, CONV_A_WIDTH, cs)
        z = proj_s[:, 3 * D_CONV + j * LANES:3 * D_CONV + (j + 1) * LANES]
        ya_s[:, cs] = (proj_s[:, cs] * conv * _silu(z)).astype(BF16)

    def group_b(j):
        cs = slice(j * LANES, (j + 1) * LANES)
        a = _silu(causal_conv(qc_s, cwq_ref, CONV_QKV_WIDTH, cs))
        if j < 2 * N_HEADS:
            rn = lax.rsqrt(jnp.sum(a * a, axis=-1, keepdims=True) + L2_EPS)
            if j < N_HEADS:
                rn = rn * (HEAD_DIM ** -0.5)
            a = a * rn
        qkv_s[:, cs] = a

    def gates():
        xs = gate_s[:, 0:LANES] + dtb_ref[...]
        softplus = jnp.maximum(xs, 0.0) + jnp.log1p(jnp.exp(-jnp.abs(xs)))
        g_s[...] = -jnp.exp(alog_ref[...]) * softplus
        beta_s[...] = 1.0 / (1.0 + jnp.exp(-gate_s[:, 0:LANES]))

    def silu_zg():
        zg_s[...] = _silu(zraw_s[...])

    return {"gates": gates, "conv_in": conv_in, "a": group_a, "b": group_b, "silu_zg": silu_zg}


def _layer_kernel(*refs, bb, tb, c, n_t, n_tiles, has_init):
    refs = list(refs)
    xprev_ref, xnext_ref = refs.pop(0), refs.pop(0)
    if has_init:
        bufa_ref, bufq_ref, s0_ref = refs.pop(0), refs.pop(0), refs.pop(0)
    (lnpre_ref, win_ref, cwa_ref, cwq_ref, alog_ref, dtb_ref, gnw_ref, wout_ref, lnpost_ref,
     y_ref, na_ref, nq_ref, ns_ref,
     w_s, h_s, proj_s, ua_s, qc_s, zraw_s, gate_s, qkv_s, zg_s, g_s, beta_s, ya_s, mix_s, out_s) = refs
    nc = tb // c
    s = pl.program_id(0)
    proj_bufs = (w_s, proj_s, qc_s, zraw_s, gate_s)
    pieces = _elementwise_pieces(bb, tb, cwa_ref, cwq_ref, alog_ref, dtb_ref, proj_s, ua_s, qc_s,
                                 zraw_s, gate_s, qkv_s, zg_s, ya_s, g_s, beta_s)
    tail_rows = lambda b, width: slice(b * (PAD + tb) + PAD - (width - 1), b * (PAD + tb) + PAD)
    last_rows = lambda b, width: slice((b + 1) * (PAD + tb) - (width - 1), (b + 1) * (PAD + tb))

    @pl.when(s == 0)
    def _first_tile_early_pieces():
        for j in range(D_MAIN // PROJ_NBLK):
            cols = slice(j * PROJ_NBLK, (j + 1) * PROJ_NBLK)
            w_s[:, cols] = win_ref[:, cols].astype(BF16)
        w_s[:, D_MAIN:D_MAIN + GATE_W] = jnp.zeros((D_MODEL, GATE_W), BF16)
        w_s[:, D_MAIN:D_MAIN + 2 * N_HEADS] = win_ref[:, D_MAIN:D_MAIN + 2 * N_HEADS].astype(BF16)
        hb = _normalised_input(xprev_ref, lnpre_ref)
        h_s[...] = hb
        for j in EARLY_PIECES:
            _project_piece(bb, tb, j, hb, *proj_bufs)
        mix_s[...] = jnp.zeros(mix_s.shape, BF16)

    @pl.when((s % n_t == 0) & (s < n_tiles))
    def _load_stream_state():
        for b in range(bb):
            if has_init:
                ua_s[tail_rows(b, CONV_A_WIDTH), :] = bufa_ref[0, b]
                qc_s[tail_rows(b, CONV_QKV_WIDTH), :] = bufq_ref[0, b]
                ns_ref[0, b] = s0_ref[0, b]
            else:
                ua_s[tail_rows(b, CONV_A_WIDTH), :] = jnp.zeros((CONV_A_WIDTH - 1, D_CONV), F32)
                qc_s[tail_rows(b, CONV_QKV_WIDTH), :] = jnp.zeros((CONV_QKV_WIDTH - 1, D_QKV), F32)
                ns_ref[0, b] = jnp.zeros((N_HEADS, HEAD_DIM, HEAD_DIM), F32)

    hb = h_s[...]
    for kind, *args in EARLY_FOLLOWERS:
        pieces[kind](*args)

    def project_pieces(schedule):
        for j, followers in schedule:
            _project_piece(bb, tb, j, hb, *proj_bufs)
            for kind, *args in followers:
                pieces[kind](*args)
            yield

    _interleave([project_pieces(LATE_FIRST)])

    row = lax.broadcasted_iota(jnp.int32, (c, c), 0)
    col = lax.broadcasted_iota(jnp.int32, (c, c), 1)
    incl = jnp.where(row >= col, 0.0, -jnp.inf)
    strict = jnp.where(row > col, 0.0, -jnp.inf)
    eye = jnp.where(row == col, 1.0, 0.0)
    row3 = lax.broadcasted_iota(jnp.int32, (c, 3 * c), 0)
    col3 = lax.broadcasted_iota(jnp.int32, (c, 3 * c), 1) & (c - 1)
    cumsum_mat = jnp.where(row3 >= col3, 1.0, 0.0).astype(BF16)

    pre = {}
    decay_last = {}
    chunks = [(b, n, slice(b * tb + n * c, b * tb + (n + 1) * c))
              for b in range(bb) for n in range(nc)]
    gate_terms = {}
    for b, n, rs in chunks:
        gb = _dot(cumsum_mat, _split3(g_s[rs, :]))
        gl = gb[c - 1:c, :]
        gb_rows = jnp.concatenate([gb, jnp.zeros((LANES - c, LANES), F32)], axis=0).T
        gate_terms[b, n] = (gb, gb_rows, jnp.exp(gb), jnp.exp(gl - gb), jnp.exp(gl), beta_s[rs, :])

    def head_group_chains(start_round, heads):
        for _ in range(start_round):
            yield
        chains = []
        for b, n, rs in chunks:
            gb, gb_rows, e_gb, e_rem, e_gl, beta_c = gate_terms[b, n]
            for hd in heads:
                qh = qkv_s[rs, hd * HEAD_DIM:(hd + 1) * HEAD_DIM]
                kh = qkv_s[rs, D_GDN + hd * HEAD_DIM:D_GDN + (hd + 1) * HEAD_DIM]
                load_v = functools.partial(
                    lambda rows, h: qkv_s[rows, 2 * D_GDN + h * HEAD_DIM:2 * D_GDN + (h + 1) * HEAD_DIM],
                    rs, hd)
                diff = gb[:, hd:hd + 1] - gb_rows[hd:hd + 1, 0:c]
                decay_last[b, n, hd] = e_gl[:, hd:hd + 1]
                chains.append(_chunk_head_precompute(
                    pre, (b, n, hd), c, qh, kh, load_v, beta_c[:, N_HEADS + hd:N_HEADS + hd + 1],
                    e_gb[:, hd:hd + 1],
                    e_rem[:, hd:hd + 1], diff, incl, strict, eye))
        yield from _round_robin(chains)

    def previous_tile_output():
        mix = mix_s[...]
        this_round = 0
        for j, at_round in enumerate(OUT_PROJ_ROUNDS):
            while this_round < at_round:
                yield
                this_round += 1
            cols = slice(j * PROJ_NBLK, (j + 1) * PROJ_NBLK)
            out_s[:, cols] = _dot(mix, wout_ref[:, cols])
        yield
        out = out_s[...]
        out_var = jnp.mean(out * out, axis=-1, keepdims=True)
        y = (xprev_ref[...].reshape(bb * tb, D_MODEL)
             + out * lax.rsqrt(out_var + RMS_EPS) * lnpost_ref[...])
        y_ref[...] = y.reshape(bb, tb, D_MODEL)

    _interleave([project_pieces(LATE_REST), previous_tile_output()]
                + [head_group_chains(start, heads) for start, heads in HEAD_GROUPS])

    def finish_head(rs, hd):
        def finish(o):
            o_var = jnp.mean(o * o, axis=-1, keepdims=True)
            o = o * lax.rsqrt(o_var + RMS_EPS) * gnw_ref[...]
            mix_s[rs, D_CONV + hd * HEAD_DIM:D_CONV + (hd + 1) * HEAD_DIM] = (
                o * zg_s[rs, hd * HEAD_DIM:(hd + 1) * HEAD_DIM]).astype(BF16)
        return finish

    state = {(b, hd): ns_ref[0, b, hd] for b in range(bb) for hd in range(N_HEADS)}
    mix_s[:, 0:D_CONV] = ya_s[...]

    def recurrence():
        for n in range(nc):
            yield from _round_robin([
                _chunk_head_recurrence(pre[b, n, hd], state, (b, hd), decay_last[b, n, hd],
                                       finish_head(slice(b * tb + n * c, b * tb + (n + 1) * c), hd))
                for b in range(bb) for hd in range(N_HEADS)])

    def next_tile_early_pieces():
        hb_next = _normalised_input(xnext_ref, lnpre_ref)
        h_s[...] = hb_next
        for j in EARLY_PIECES:
            _project_piece(bb, tb, j, hb_next, *proj_bufs)
            yield

    _interleave([recurrence(), next_tile_early_pieces()])

    @pl.when(s < n_tiles)
    def _carry_stream_state():
        for (b, hd), value in state.items():
            ns_ref[0, b, hd] = value
        for b in range(bb):
            tail_a = ua_s[last_rows(b, CONV_A_WIDTH), :]
            ua_s[tail_rows(b, CONV_A_WIDTH), :] = tail_a
            na_ref[0, b] = tail_a
            tail_q = qc_s[last_rows(b, CONV_QKV_WIDTH), :]
            qc_s[tail_rows(b, CONV_QKV_WIDTH), :] = tail_q
            nq_ref[0, b] = tail_q


def _layer_call(x, init, params, *, bb, tb, c, name):
    bsz, seq, _ = x.shape
    assert bsz % bb == 0 and seq % tb == 0 and tb % c == 0 and tb % SUBLANES == 0 and c <= LANES
    has_init = init is not None
    n_t = seq // tb
    n_tiles = (bsz // bb) * n_t
    rows = bb * tb

    def this_tile(s):
        return jnp.minimum(s, n_tiles - 1)

    def prev_tile(s):
        return jnp.maximum(s - 1, 0)

    def next_tile(s):
        return jnp.minimum(s + 1, n_tiles - 1)

    tile_spec = lambda tile: pl.BlockSpec((bb, tb, D_MODEL), lambda s: (tile(s) // n_t, tile(s) % n_t, 0))
    stream_specs = [
        pl.BlockSpec((1, bb, CONV_A_WIDTH - 1, D_CONV), lambda s: (0, this_tile(s) // n_t, 0, 0)),
        pl.BlockSpec((1, bb, CONV_QKV_WIDTH - 1, D_QKV), lambda s: (0, this_tile(s) // n_t, 0, 0)),
        pl.BlockSpec((1, bb, N_HEADS, HEAD_DIM, HEAD_DIM),
                     lambda s: (0, this_tile(s) // n_t, 0, 0, 0)),
    ]
    in_specs = [tile_spec(prev_tile), tile_spec(next_tile)]
    operands = [x, x]
    if has_init:
        in_specs += stream_specs
        operands += list(init)
    for i, p in enumerate(params):
        mode = {"pipeline_mode": pl.Buffered(1)} if i in SINGLE_BUFFERED_PARAMS else {}
        in_specs.append(pl.BlockSpec(p.shape, lambda s, nd=p.ndim: (0,) * nd, **mode))
    operands += list(params)
    out_shape = (
        jax.ShapeDtypeStruct(x.shape, x.dtype),
        jax.ShapeDtypeStruct((1, bsz, CONV_A_WIDTH - 1, D_CONV), F32),
        jax.ShapeDtypeStruct((1, bsz, CONV_QKV_WIDTH - 1, D_QKV), F32),
        jax.ShapeDtypeStruct((1, bsz, N_HEADS, HEAD_DIM, HEAD_DIM), F32),
    )
    out_specs = [tile_spec(prev_tile)] + stream_specs
    scratch = [
        pltpu.VMEM((D_MODEL, D_MAIN + GATE_W), BF16),
        pltpu.VMEM((rows, D_MODEL), BF16),
        pltpu.VMEM((rows, D_A), F32),
        pltpu.VMEM((bb * (PAD + tb), D_CONV), F32),
        pltpu.VMEM((bb * (PAD + tb), D_QKV), F32),
        pltpu.VMEM((rows, D_GDN), F32),
        pltpu.VMEM((rows, GATE_W), F32),
        pltpu.VMEM((rows, D_QKV), F32),
        pltpu.VMEM((rows, D_GDN), F32),
        pltpu.VMEM((rows, LANES), F32),
        pltpu.VMEM((rows, LANES), F32),
        pltpu.VMEM((rows, D_CONV), BF16),
        pltpu.VMEM((rows, D_MIX), BF16),
        pltpu.VMEM((rows, D_MODEL), F32),
    ]
    return pl.pallas_call(
        functools.partial(_layer_kernel, bb=bb, tb=tb, c=c, n_t=n_t, n_tiles=n_tiles,
                          has_init=has_init),
        grid=(n_tiles + 1,),
        in_specs=in_specs,
        out_specs=out_specs,
        out_shape=out_shape,
        scratch_shapes=scratch,
        compiler_params=pltpu.CompilerParams(
            dimension_semantics=("arbitrary",),
            vmem_limit_bytes=VMEM_LIMIT_BYTES),
        name=name,
    )(*operands)


def _tile_shape(bsz, seq):
    tb = min(seq, TILE_ROWS)
    bb = max(1, math.gcd(bsz, TILE_ROWS // tb))
    return bb, tb


def _prepare_params(ln_pre, w_in, conv_a_w, conv_qkv_w, a_log, dt_bias, gdn_norm_w, w_out, ln_post):
    pad_heads = lambda v: jnp.pad(v, (0, LANES - N_HEADS)).reshape(1, LANES)
    return (ln_pre.reshape(1, D_MODEL), w_in, conv_a_w, conv_qkv_w,
            pad_heads(a_log), pad_heads(dt_bias), gdn_norm_w.reshape(1, HEAD_DIM),
            w_out.astype(BF16), ln_post.reshape(1, D_MODEL))


def kernel(x_prompt, x_sample, cache_conv_a, cache_conv_qkv, state_gdn, ln_pre, w_in, conv_a_w,
           conv_qkv_w, a_log, dt_bias, gdn_norm_w, w_out, ln_post):
    assert ln_pre.shape[0] == 1, "single-layer kernel"
    params = _prepare_params(ln_pre[0], w_in[0], conv_a_w[0], conv_qkv_w[0], a_log[0], dt_bias[0],
                             gdn_norm_w[0], w_out[0], ln_post[0])
    bb_p, tb_p = _tile_shape(*x_prompt.shape[:2])
    bb_s, tb_s = _tile_shape(*x_sample.shape[:2])
    y_p, a_p, q_p, s_p = _layer_call(x_prompt, None, params, bb=bb_p, tb=tb_p,
                                     c=math.gcd(x_prompt.shape[1], CHUNK), name="hybrid_prompt")
    y_s, a_s, q_s, s_s = _layer_call(x_sample, (cache_conv_a, cache_conv_qkv, state_gdn), params,
                                     bb=bb_s, tb=tb_s, c=math.gcd(x_sample.shape[1], CHUNK),
                                     name="hybrid_sample")
    return (y_p, y_s, a_p, q_p, s_p, a_s, q_s, s_s)
```

```python
import functools
import math

import jax
import jax.numpy as jnp
from jax import lax
from jax.experimental import pallas as pl
from jax.experimental.pallas import tpu as pltpu

D_MODEL = 1024
D_CONV = 512
CONV_A_WIDTH = 3
N_HEADS = 4
HEAD_DIM = 128
D_GDN = N_HEADS * HEAD_DIM
D_QKV = 3 * D_GDN
CONV_QKV_WIDTH = 4
D_MIX = D_CONV + D_GDN
D_A = 4 * D_CONV
D_MAIN = D_A + D_QKV + D_GDN
CHUNK = 64
RMS_EPS = 1e-6
L2_EPS = 1e-6

LANES = 128
SUBLANES = 8
PAD = SUBLANES
GATE_W = 2 * LANES
PROJ_NBLK = 256
TILE_ROWS = 256
VMEM_LIMIT_BYTES = 52 * 1024 * 1024

EARLY_PIECES = (16, 2, 3, 4, 5, 0, 6, 1, 7)
EARLY_FOLLOWERS = (("gates",), ("conv_in",), ("a", 0), ("a", 1), ("a", 2), ("a", 3))
_QKV_PIECE = lambda m: (D_A // PROJ_NBLK + m, (("b", 2 * m), ("b", 2 * m + 1)))
LATE_FIRST = (_QKV_PIECE(0), _QKV_PIECE(2))
LATE_REST = (_QKV_PIECE(1), _QKV_PIECE(3), _QKV_PIECE(4), _QKV_PIECE(5),
             (14, ()), (15, (("silu_zg",),)))
HEAD_GROUPS = ((0, (0, 1)), (2, (2, 3)))
OUT_PROJ_ROUNDS = (0, 0, 3, 3)

F32 = jnp.float32
BF16 = jnp.bfloat16


def _dot(a, b):
    return jnp.dot(a, b, preferred_element_type=F32)


def _dot_nt(a, b):
    return lax.dot_general(a, b, (((1,), (1,)), ((), ())), preferred_element_type=F32)


def _dot_tn(a, b):
    return lax.dot_general(a, b, (((0,), (0,)), ((), ())), preferred_element_type=F32)


def _silu_of_half(h):
    return h + h * jnp.tanh(h)


def _silu(z):
    return _silu_of_half(0.5 * z)


def _split3(x):
    hi = x.astype(BF16)
    r1 = x - hi.astype(F32)
    mid = r1.astype(BF16)
    lo = (r1 - mid.astype(F32)).astype(BF16)
    return jnp.concatenate([hi, mid, lo], axis=0)


def _round_robin(generators):
    alive = list(generators)
    while alive:
        still = []
        for g in alive:
            try:
                next(g)
                still.append(g)
            except StopIteration:
                pass
        alive = still
        yield


def _interleave(generators):
    for _ in _round_robin(generators):
        pass


def _chunk_head_precompute(out, key, c, qh, kh, load_v, bcol, ecol, rcol, diff, incl, strict, eye):
    decay = jnp.exp(diff + incl)
    kb = kh.astype(BF16)
    scores = _dot_nt(jnp.concatenate([kb, qh.astype(BF16)], axis=0), kb)
    qg = (qh * ecol).astype(BF16)
    k_rem = (kh * rcol).astype(BF16)
    k_rhs = kh * (bcol * ecol)
    yield
    a_mat = scores[:c] * (decay * strict) * bcol
    attn = (scores[c:] * decay).astype(BF16)
    levels = int(math.log2(c)) - 1
    ab = a_mat.astype(BF16)
    x = _dot(ab, ab)
    yield
    t = eye - a_mat
    for k in range(1, levels + 1):
        xb = x.astype(BF16)
        if k < levels:
            tx = _dot(jnp.concatenate([t.astype(BF16), xb], axis=0), xb)
            yield
            t = t + tx[:c]
            x = tx[c:]
        else:
            tx = _dot(t.astype(BF16), xb)
            yield
            t = t + tx
    rhs = jnp.concatenate([load_v() * bcol, k_rhs], axis=1)
    uw = _dot(t.astype(BF16), rhs.astype(BF16))
    yield
    out[key] = dict(u=uw[:, :HEAD_DIM], w=uw[:, HEAD_DIM:].astype(BF16), qg=qg, attn=attn,
                    k_rem=k_rem)


def _chunk_head_recurrence(pre, state, key, decay_last, finish):
    s_old = state[key]
    c = pre["u"].shape[0]
    wq_s = _dot(jnp.concatenate([pre["w"], pre["qg"]], axis=0), s_old.astype(BF16))
    yield
    v_new = (pre["u"] - wq_s[:c]).astype(BF16)
    o = wq_s[c:] + _dot(pre["attn"], v_new)
    state[key] = s_old * decay_last + _dot_tn(pre["k_rem"], v_new)
    yield
    finish(o)


def _normalised_input(x_ref, lnpre_ref):
    bb, tb, _ = x_ref.shape
    x = x_ref[...].reshape(bb * tb, D_MODEL)
    var = jnp.mean(x * x, axis=-1, keepdims=True)
    return (x * lax.rsqrt(var + RMS_EPS) * lnpre_ref[...]).astype(BF16)


def _staged_rows(b, tb, back=0):
    first = b * (PAD + tb) + PAD - back
    return slice(first, first + tb)


def _project_piece(bb, tb, j, hb, win_ref, proj_s, qc_s, zraw_s, gate_s):
    lo = j * PROJ_NBLK
    hi = (j + 1) * PROJ_NBLK
    res = _dot(hb, win_ref[:, lo:hi])
    if lo < D_A:
        proj_s[:, lo:hi] = res
    elif lo < D_A + D_QKV:
        for b in range(bb):
            qc_s[_staged_rows(b, tb), lo - D_A:hi - D_A] = res[b * tb:(b + 1) * tb]
    elif lo < D_MAIN:
        zraw_s[:, lo - D_A - D_QKV:hi - D_A - D_QKV] = res
    else:
        gate_s[...] = res


def _elementwise_pieces(bb, tb, cwa_ref, cwq_ref, alog_ref, dtb_ref, proj_s, ua_s, qc_s, zraw_s,
                        gate_s, qkv_s, zg_s, ya_s, g_s, beta_s):
    def causal_conv(buf_s, w_ref, width, cs, halved=False):
        taps = [w_ref[k:k + 1, cs] for k in range(width)]
        if halved:
            taps = [0.5 * w for w in taps]
        rows = []
        for b in range(bb):
            conv = taps[0] * buf_s[_staged_rows(b, tb, width - 1), cs]
            for k in range(1, width):
                conv = conv + taps[k] * buf_s[_staged_rows(b, tb, width - 1 - k), cs]
            rows.append(conv)
        return rows[0] if bb == 1 else jnp.concatenate(rows, axis=0)

    def conv_in():
        for j in range(D_CONV // LANES):
            cs = slice(j * LANES, (j + 1) * LANES)
            u = (proj_s[:, D_CONV + j * LANES:D_CONV + (j + 1) * LANES]
                 * proj_s[:, 2 * D_CONV + j * LANES:2 * D_CONV + (j + 1) * LANES])
            for b in range(bb):
                ua_s[_staged_rows(b, tb), cs] = u[b * tb:(b + 1) * tb]

    def group_a(j):
        cs = slice(j * LANES, (j + 1) * LANES)
        conv = causal_conv(ua_s, cwa_ref, CONV_A_WIDTH, cs)
        z = proj_s[:, 3 * D_CONV + j * LANES:3 * D_CONV + (j + 1) * LANES]
        ya_s[:, cs] = (proj_s[:, cs] * conv * _silu(z)).astype(BF16)

    def group_b(j):
        cs = slice(j * LANES, (j + 1) * LANES)
        a = _silu_of_half(causal_conv(qc_s, cwq_ref, CONV_QKV_WIDTH, cs, halved=True))
        if j < 2 * N_HEADS:
            rn = lax.rsqrt(jnp.sum(a * a, axis=-1, keepdims=True) + L2_EPS)
            if j < N_HEADS:
                rn = rn * (HEAD_DIM ** -0.5)
            a = a * rn
        qkv_s[:, cs] = a

    def gates():
        xs = gate_s[:, 0:LANES] + dtb_ref[...]
        softplus = jnp.maximum(xs, 0.0) + jnp.log1p(jnp.exp(-jnp.abs(xs)))
        g_s[...] = -jnp.exp(alog_ref[...]) * softplus
        beta_s[...] = 1.0 / (1.0 + jnp.exp(-gate_s[:, LANES:2 * LANES]))

    def silu_zg():
        zg_s[...] = _silu(zraw_s[...])

    return {"gates": gates, "conv_in": conv_in, "a": group_a, "b": group_b, "silu_zg": silu_zg}


def _layer_kernel(*refs, bb, tb, c, n_t, n_tiles, has_init):
    refs = list(refs)
    xprev_ref, xnext_ref = refs.pop(0), refs.pop(0)
    if has_init:
        bufa_ref, bufq_ref, s0_ref = refs.pop(0), refs.pop(0), refs.pop(0)
    (lnpre_ref, win_ref, cwa_ref, cwq_ref, alog_ref, dtb_ref, gnw_ref, wout_ref, lnpost_ref,
     y_ref, na_ref, nq_ref, ns_ref,
     h_s, proj_s, ua_s, qc_s, zraw_s, gate_s, qkv_s, zg_s, g_s, beta_s, ya_s, mix_s, out_s) = refs
    nc = tb // c
    s = pl.program_id(0)
    proj_bufs = (win_ref, proj_s, qc_s, zraw_s, gate_s)
    pieces = _elementwise_pieces(bb, tb, cwa_ref, cwq_ref, alog_ref, dtb_ref, proj_s, ua_s, qc_s,
                                 zraw_s, gate_s, qkv_s, zg_s, ya_s, g_s, beta_s)
    tail_rows = lambda b, width: slice(b * (PAD + tb) + PAD - (width - 1), b * (PAD + tb) + PAD)
    last_rows = lambda b, width: slice((b + 1) * (PAD + tb) - (width - 1), (b + 1) * (PAD + tb))

    @pl.when(s == 0)
    def _first_tile_early_pieces():
        hb = _normalised_input(xprev_ref, lnpre_ref)
        h_s[...] = hb
        for j in EARLY_PIECES:
            _project_piece(bb, tb, j, hb, *proj_bufs)
        mix_s[...] = jnp.zeros(mix_s.shape, BF16)

    @pl.when((s % n_t == 0) & (s < n_tiles))
    def _load_stream_state():
        for b in range(bb):
            if has_init:
                ua_s[tail_rows(b, CONV_A_WIDTH), :] = bufa_ref[0, b]
                qc_s[tail_rows(b, CONV_QKV_WIDTH), :] = bufq_ref[0, b]
                ns_ref[0, b] = s0_ref[0, b]
            else:
                ua_s[tail_rows(b, CONV_A_WIDTH), :] = jnp.zeros((CONV_A_WIDTH - 1, D_CONV), F32)
                qc_s[tail_rows(b, CONV_QKV_WIDTH), :] = jnp.zeros((CONV_QKV_WIDTH - 1, D_QKV), F32)
                ns_ref[0, b] = jnp.zeros((N_HEADS, HEAD_DIM, HEAD_DIM), F32)

    hb = h_s[...]
    for kind, *args in EARLY_FOLLOWERS:
        pieces[kind](*args)

    def project_pieces(schedule):
        for j, followers in schedule:
            _project_piece(bb, tb, j, hb, *proj_bufs)
            for kind, *args in followers:
                pieces[kind](*args)
            yield

    _interleave([project_pieces(LATE_FIRST)])

    row = lax.broadcasted_iota(jnp.int32, (c, c), 0)
    col = lax.broadcasted_iota(jnp.int32, (c, c), 1)
    incl = jnp.where(row >= col, 0.0, -jnp.inf)
    strict = jnp.where(row > col, 1.0, 0.0)
    eye = jnp.where(row == col, 1.0, 0.0)
    row3 = lax.broadcasted_iota(jnp.int32, (c, 3 * c), 0)
    col3 = lax.broadcasted_iota(jnp.int32, (c, 3 * c), 1) & (c - 1)
    cumsum_mat = jnp.where(row3 >= col3, 1.0, 0.0).astype(BF16)

    pre = {}
    decay_last = {}
    chunks = [(b, n, slice(b * tb + n * c, b * tb + (n + 1) * c))
              for b in range(bb) for n in range(nc)]
    gate_terms = {}
    for b, n, rs in chunks:
        gb = _dot(cumsum_mat, _split3(g_s[rs, :]))
        gl = gb[c - 1:c, :]
        gb_rows = jnp.concatenate([gb, jnp.zeros((LANES - c, LANES), F32)], axis=0).T
        gate_terms[b, n] = (gb, gb_rows, jnp.exp(gb), jnp.exp(gl - gb), jnp.exp(gl), beta_s[rs, :])

    def head_group_chains(start_round, heads):
        for _ in range(start_round):
            yield
        chains = []
        for b, n, rs in chunks:
            gb, gb_rows, e_gb, e_rem, e_gl, beta_c = gate_terms[b, n]
            for hd in heads:
                qh = qkv_s[rs, hd * HEAD_DIM:(hd + 1) * HEAD_DIM]
                kh = qkv_s[rs, D_GDN + hd * HEAD_DIM:D_GDN + (hd + 1) * HEAD_DIM]
                load_v = functools.partial(
                    lambda rows, h: qkv_s[rows, 2 * D_GDN + h * HEAD_DIM:2 * D_GDN + (h + 1) * HEAD_DIM],
                    rs, hd)
                diff = gb[:, hd:hd + 1] - gb_rows[hd:hd + 1, 0:c]
                decay_last[b, n, hd] = e_gl[:, hd:hd + 1]
                chains.append(_chunk_head_precompute(
                    pre, (b, n, hd), c, qh, kh, load_v, beta_c[:, hd:hd + 1], e_gb[:, hd:hd + 1],
                    e_rem[:, hd:hd + 1], diff, incl, strict, eye))
        yield from _round_robin(chains)

    def previous_tile_output():
        mix = mix_s[...]
        this_round = 0
        for j, at_round in enumerate(OUT_PROJ_ROUNDS):
            while this_round < at_round:
                yield
                this_round += 1
            cols = slice(j * PROJ_NBLK, (j + 1) * PROJ_NBLK)
            out_s[:, cols] = _dot(mix, wout_ref[:, cols])
        yield
        out = out_s[...]
        out_var = jnp.mean(out * out, axis=-1, keepdims=True)
        y = (xprev_ref[...].reshape(bb * tb, D_MODEL)
             + out * lax.rsqrt(out_var + RMS_EPS) * lnpost_ref[...])
        y_ref[...] = y.reshape(bb, tb, D_MODEL)

    _interleave([project_pieces(LATE_REST), previous_tile_output()]
                + [head_group_chains(start, heads) for start, heads in HEAD_GROUPS])

    def finish_head(rs, hd):
        def finish(o):
            o_var = jnp.mean(o * o, axis=-1, keepdims=True)
            o = o * lax.rsqrt(o_var + RMS_EPS) * gnw_ref[...]
            mix_s[rs, D_CONV + hd * HEAD_DIM:D_CONV + (hd + 1) * HEAD_DIM] = (
                o * zg_s[rs, hd * HEAD_DIM:(hd + 1) * HEAD_DIM]).astype(BF16)
        return finish

    state = {(b, hd): ns_ref[0, b, hd] for b in range(bb) for hd in range(N_HEADS)}
    mix_s[:, 0:D_CONV] = ya_s[...]

    def recurrence():
        for n in range(nc):
            yield from _round_robin([
                _chunk_head_recurrence(pre[b, n, hd], state, (b, hd), decay_last[b, n, hd],
                                       finish_head(slice(b * tb + n * c, b * tb + (n + 1) * c), hd))
                for b in range(bb) for hd in range(N_HEADS)])

    def next_tile_early_pieces():
        hb_next = _normalised_input(xnext_ref, lnpre_ref)
        h_s[...] = hb_next
        for j in EARLY_PIECES:
            _project_piece(bb, tb, j, hb_next, *proj_bufs)
            yield

    _interleave([recurrence(), next_tile_early_pieces()])

    @pl.when(s < n_tiles)
    def _carry_stream_state():
        for (b, hd), value in state.items():
            ns_ref[0, b, hd] = value
        for b in range(bb):
            tail_a = ua_s[last_rows(b, CONV_A_WIDTH), :]
            ua_s[tail_rows(b, CONV_A_WIDTH), :] = tail_a
            na_ref[0, b] = tail_a
            tail_q = qc_s[last_rows(b, CONV_QKV_WIDTH), :]
            qc_s[tail_rows(b, CONV_QKV_WIDTH), :] = tail_q
            nq_ref[0, b] = tail_q


def _layer_call(x, init, params, *, bb, tb, c, name):
    bsz, seq, _ = x.shape
    assert bsz % bb == 0 and seq % tb == 0 and tb % c == 0 and tb % SUBLANES == 0 and c <= LANES
    has_init = init is not None
    n_t = seq // tb
    n_tiles = (bsz // bb) * n_t
    rows = bb * tb

    def this_tile(s):
        return jnp.minimum(s, n_tiles - 1)

    def prev_tile(s):
        return jnp.maximum(s - 1, 0)

    def next_tile(s):
        return jnp.minimum(s + 1, n_tiles - 1)

    tile_spec = lambda tile: pl.BlockSpec((bb, tb, D_MODEL), lambda s: (tile(s) // n_t, tile(s) % n_t, 0))
    stream_specs = [
        pl.BlockSpec((1, bb, CONV_A_WIDTH - 1, D_CONV), lambda s: (0, this_tile(s) // n_t, 0, 0)),
        pl.BlockSpec((1, bb, CONV_QKV_WIDTH - 1, D_QKV), lambda s: (0, this_tile(s) // n_t, 0, 0)),
        pl.BlockSpec((1, bb, N_HEADS, HEAD_DIM, HEAD_DIM),
                     lambda s: (0, this_tile(s) // n_t, 0, 0, 0)),
    ]
    in_specs = [tile_spec(prev_tile), tile_spec(next_tile)]
    operands = [x, x]
    if has_init:
        in_specs += stream_specs
        operands += list(init)
    in_specs += [pl.BlockSpec(p.shape, lambda s, nd=p.ndim: (0,) * nd) for p in params]
    operands += list(params)
    out_shape = (
        jax.ShapeDtypeStruct(x.shape, x.dtype),
        jax.ShapeDtypeStruct((1, bsz, CONV_A_WIDTH - 1, D_CONV), F32),
        jax.ShapeDtypeStruct((1, bsz, CONV_QKV_WIDTH - 1, D_QKV), F32),
        jax.ShapeDtypeStruct((1, bsz, N_HEADS, HEAD_DIM, HEAD_DIM), F32),
    )
    out_specs = [tile_spec(prev_tile)] + stream_specs
    scratch = [
        pltpu.VMEM((rows, D_MODEL), BF16),
        pltpu.VMEM((rows, D_A), F32),
        pltpu.VMEM((bb * (PAD + tb), D_CONV), F32),
        pltpu.VMEM((bb * (PAD + tb), D_QKV), F32),
        pltpu.VMEM((rows, D_GDN), F32),
        pltpu.VMEM((rows, GATE_W), F32),
        pltpu.VMEM((rows, D_QKV), F32),
        pltpu.VMEM((rows, D_GDN), F32),
        pltpu.VMEM((rows, LANES), F32),
        pltpu.VMEM((rows, LANES), F32),
        pltpu.VMEM((rows, D_CONV), BF16),
        pltpu.VMEM((rows, D_MIX), BF16),
        pltpu.VMEM((rows, D_MODEL), F32),
    ]
    return pl.pallas_call(
        functools.partial(_layer_kernel, bb=bb, tb=tb, c=c, n_t=n_t, n_tiles=n_tiles,
                          has_init=has_init),
        grid=(n_tiles + 1,),
        in_specs=in_specs,
        out_specs=out_specs,
        out_shape=out_shape,
        scratch_shapes=scratch,
        compiler_params=pltpu.CompilerParams(
            dimension_semantics=("arbitrary",),
            vmem_limit_bytes=VMEM_LIMIT_BYTES),
        name=name,
    )(*operands)


def _tile_shape(bsz, seq):
    tb = min(seq, TILE_ROWS)
    bb = max(1, math.gcd(bsz, TILE_ROWS // tb))
    return bb, tb


def _prepare_params(ln_pre, w_in, conv_a_w, conv_qkv_w, a_log, dt_bias, gdn_norm_w, w_out, ln_post):
    w = w_in.astype(BF16)
    gap = jnp.zeros((D_MODEL, LANES - N_HEADS), BF16)
    w_all = jnp.concatenate([w[:, :D_MAIN + N_HEADS], gap, w[:, D_MAIN + N_HEADS:], gap], axis=1)
    pad_heads = lambda v: jnp.pad(v, (0, LANES - N_HEADS)).reshape(1, LANES)
    return (ln_pre.reshape(1, D_MODEL), w_all, conv_a_w, conv_qkv_w,
            pad_heads(a_log), pad_heads(dt_bias), gdn_norm_w.reshape(1, HEAD_DIM),
            w_out.astype(BF16), ln_post.reshape(1, D_MODEL))


def kernel(x_prompt, x_sample, cache_conv_a, cache_conv_qkv, state_gdn, ln_pre, w_in, conv_a_w,
           conv_qkv_w, a_log, dt_bias, gdn_norm_w, w_out, ln_post):
    assert ln_pre.shape[0] == 1, "single-layer kernel"
    params = _prepare_params(ln_pre[0], w_in[0], conv_a_w[0], conv_qkv_w[0], a_log[0], dt_bias[0],
                             gdn_norm_w[0], w_out[0], ln_post[0])
    bb_p, tb_p = _tile_shape(*x_prompt.shape[:2])
    bb_s, tb_s = _tile_shape(*x_sample.shape[:2])
    y_p, a_p, q_p, s_p = _layer_call(x_prompt, None, params, bb=bb_p, tb=tb_p,
                                     c=math.gcd(x_prompt.shape[1], CHUNK), name="hybrid_prompt")
    y_s, a_s, q_s, s_s = _layer_call(x_sample, (cache_conv_a, cache_conv_qkv, state_gdn), params,
                                     bb=bb_s, tb=tb_s, c=math.gcd(x_sample.shape[1], CHUNK),
                                     name="hybrid_sample")
    return (y_p, y_s, a_p, q_p, s_p, a_s, q_s, s_s)
```

```python
import functools
import math

import jax
import jax.numpy as jnp
from jax import lax
from jax.experimental import pallas as pl
from jax.experimental.pallas import tpu as pltpu

D_MODEL = 1024
D_CONV = 512
CONV_A_WIDTH = 3
N_HEADS = 4
HEAD_DIM = 128
D_GDN = N_HEADS * HEAD_DIM
D_QKV = 3 * D_GDN
CONV_QKV_WIDTH = 4
D_MIX = D_CONV + D_GDN
D_A = 4 * D_CONV
D_MAIN = D_A + D_QKV + D_GDN
CHUNK = 64
RMS_EPS = 1e-6
L2_EPS = 1e-6

LANES = 128
SUBLANES = 8
PAD = SUBLANES
GATE_W = 2 * LANES
PROJ_NBLK = 256
TILE_ROWS = 256
VMEM_LIMIT_BYTES = 52 * 1024 * 1024

EARLY_PIECES = (16, 2, 3, 4, 5, 0, 6, 1, 7)
EARLY_FOLLOWERS = (("gates",), ("conv_in",), ("a", 0), ("a", 1), ("a", 2), ("a", 3))
_QKV_PIECE = lambda m: (D_A // PROJ_NBLK + m, (("b", 2 * m), ("b", 2 * m + 1)))
EARLY_K_PIECES = (_QKV_PIECE(2), _QKV_PIECE(3))
LATE_FIRST = (_QKV_PIECE(0),)
LATE_REST = (_QKV_PIECE(1), _QKV_PIECE(4), _QKV_PIECE(5),
             (14, ()), (15, (("silu_zg",),)))
HEAD_GROUPS = ((0, (0, 1)), (1, (2, 3)))
OUT_PROJ_ROUNDS = (0, 0, 3, 3)

F32 = jnp.float32
BF16 = jnp.bfloat16


def _dot(a, b):
    return jnp.dot(a, b, preferred_element_type=F32)


def _dot_nt(a, b):
    return lax.dot_general(a, b, (((1,), (1,)), ((), ())), preferred_element_type=F32)


def _dot_tn(a, b):
    return lax.dot_general(a, b, (((0,), (0,)), ((), ())), preferred_element_type=F32)


def _silu_of_half(h):
    return h + h * jnp.tanh(h)


def _silu(z):
    return _silu_of_half(0.5 * z)


def _split3(x):
    hi = x.astype(BF16)
    r1 = x - hi.astype(F32)
    mid = r1.astype(BF16)
    lo = (r1 - mid.astype(F32)).astype(BF16)
    return jnp.concatenate([hi, mid, lo], axis=0)


def _round_robin(generators):
    alive = list(generators)
    while alive:
        still = []
        for g in alive:
            try:
                next(g)
                still.append(g)
            except StopIteration:
                pass
        alive = still
        yield


def _interleave(generators):
    for _ in _round_robin(generators):
        pass


def _chunk_head_precompute(out, key, c, qh, kh, load_v, bcol, ecol, rcol, diff, incl, strict, eye):
    decay = jnp.exp(diff + incl)
    kb = kh.astype(BF16)
    scores = _dot_nt(jnp.concatenate([kb, qh.astype(BF16)], axis=0), kb)
    qg = (qh * ecol).astype(BF16)
    k_rem = (kh * rcol).astype(BF16)
    k_rhs = kh * (bcol * ecol)
    yield
    a_mat = scores[:c] * (decay * strict) * bcol
    attn = (scores[c:] * decay).astype(BF16)
    levels = int(math.log2(c)) - 1
    ab = a_mat.astype(BF16)
    x = _dot(ab, ab)
    yield
    t = eye - a_mat
    for k in range(1, levels + 1):
        xb = x.astype(BF16)
        if k < levels:
            tx = _dot(jnp.concatenate([t.astype(BF16), xb], axis=0), xb)
            yield
            t = t + tx[:c]
            x = tx[c:]
        else:
            tx = _dot(t.astype(BF16), xb)
            yield
            t = t + tx
    rhs = jnp.concatenate([load_v() * bcol, k_rhs], axis=1)
    uw = _dot(t.astype(BF16), rhs.astype(BF16))
    yield
    out[key] = dict(u=uw[:, :HEAD_DIM], w=uw[:, HEAD_DIM:].astype(BF16), qg=qg, attn=attn,
                    k_rem=k_rem)


def _chunk_head_recurrence(pre, state, key, decay_last, finish):
    s_old = state[key]
    c = pre["u"].shape[0]
    wq_s = _dot(jnp.concatenate([pre["w"], pre["qg"]], axis=0), s_old.astype(BF16))
    yield
    v_new = (pre["u"] - wq_s[:c]).astype(BF16)
    o = wq_s[c:] + _dot(pre["attn"], v_new)
    state[key] = s_old * decay_last + _dot_tn(pre["k_rem"], v_new)
    yield
    finish(o)


def _normalised_input(x_ref, lnpre_ref):
    bb, tb, _ = x_ref.shape
    x = x_ref[...].reshape(bb * tb, D_MODEL)
    var = jnp.mean(x * x, axis=-1, keepdims=True)
    return (x * lax.rsqrt(var + RMS_EPS) * lnpre_ref[...]).astype(BF16)


def _staged_rows(b, tb, back=0):
    first = b * (PAD + tb) + PAD - back
    return slice(first, first + tb)


def _project_piece(bb, tb, j, hb, win_ref, proj_s, qc_s, zraw_s, gate_s):
    lo = j * PROJ_NBLK
    hi = (j + 1) * PROJ_NBLK
    res = _dot(hb, win_ref[:, lo:hi])
    if lo < D_A:
        proj_s[:, lo:hi] = res
    elif lo < D_A + D_QKV:
        for b in range(bb):
            qc_s[_staged_rows(b, tb), lo - D_A:hi - D_A] = res[b * tb:(b + 1) * tb]
    elif lo < D_MAIN:
        zraw_s[:, lo - D_A - D_QKV:hi - D_A - D_QKV] = res
    else:
        gate_s[...] = res


def _elementwise_pieces(bb, tb, cwa_ref, cwq_ref, alog_ref, dtb_ref, proj_s, ua_s, qc_s, zraw_s,
                        gate_s, qkv_s, zg_s, ya_s, g_s, beta_s):
    def causal_conv(buf_s, w_ref, width, cs, halved=False):
        taps = [w_ref[k:k + 1, cs] for k in range(width)]
        if halved:
            taps = [0.5 * w for w in taps]
        rows = []
        for b in range(bb):
            conv = taps[0] * buf_s[_staged_rows(b, tb, width - 1), cs]
            for k in range(1, width):
                conv = conv + taps[k] * buf_s[_staged_rows(b, tb, width - 1 - k), cs]
            rows.append(conv)
        return rows[0] if bb == 1 else jnp.concatenate(rows, axis=0)

    def conv_in():
        for j in range(D_CONV // LANES):
            cs = slice(j * LANES, (j + 1) * LANES)
            u = (proj_s[:, D_CONV + j * LANES:D_CONV + (j + 1) * LANES]
                 * proj_s[:, 2 * D_CONV + j * LANES:2 * D_CONV + (j + 1) * LANES])
            for b in range(bb):
                ua_s[_staged_rows(b, tb), cs] = u[b * tb:(b + 1) * tb]

    def group_a(j):
        cs = slice(j * LANES, (j + 1) * LANES)
        conv = causal_conv(ua_s, cwa_ref, CONV_A_WIDTH, cs)
        z = proj_s[:, 3 * D_CONV + j * LANES:3 * D_CONV + (j + 1) * LANES]
        ya_s[:, cs] = (proj_s[:, cs] * conv * _silu(z)).astype(BF16)

    def group_b(j):
        cs = slice(j * LANES, (j + 1) * LANES)
        a = _silu_of_half(causal_conv(qc_s, cwq_ref, CONV_QKV_WIDTH, cs, halved=True))
        if j < 2 * N_HEADS:
            rn = lax.rsqrt(jnp.sum(a * a, axis=-1, keepdims=True) + L2_EPS)
            if j < N_HEADS:
                rn = rn * (HEAD_DIM ** -0.5)
            a = a * rn
        qkv_s[:, cs] = a

    def gates():
        xs = gate_s[:, 0:LANES] + dtb_ref[...]
        softplus = jnp.maximum(xs, 0.0) + jnp.log1p(jnp.exp(-jnp.abs(xs)))
        g_s[...] = -jnp.exp(alog_ref[...]) * softplus
        beta_s[...] = 1.0 / (1.0 + jnp.exp(-gate_s[:, LANES:2 * LANES]))

    def silu_zg():
        zg_s[...] = _silu(zraw_s[...])

    return {"gates": gates, "conv_in": conv_in, "a": group_a, "b": group_b, "silu_zg": silu_zg}


def _layer_kernel(*refs, bb, tb, c, n_t, n_tiles, has_init):
    refs = list(refs)
    xprev_ref, xnext_ref = refs.pop(0), refs.pop(0)
    if has_init:
        bufa_ref, bufq_ref, s0_ref, bufqn_ref = refs.pop(0), refs.pop(0), refs.pop(0), refs.pop(0)
    (lnpre_ref, win_ref, cwa_ref, cwq_ref, alog_ref, dtb_ref, gnw_ref, wout_ref, lnpost_ref,
     y_ref, na_ref, nq_ref, ns_ref,
     h_s, proj_s, ua_s, qc_s, zraw_s, gate_s, qkv_s, zg_s, g_s, beta_s, ya_s, mix_s, out_s, tailq_s) = refs
    nc = tb // c
    s = pl.program_id(0)
    proj_bufs = (win_ref, proj_s, qc_s, zraw_s, gate_s)
    pieces = _elementwise_pieces(bb, tb, cwa_ref, cwq_ref, alog_ref, dtb_ref, proj_s, ua_s, qc_s,
                                 zraw_s, gate_s, qkv_s, zg_s, ya_s, g_s, beta_s)
    tail_rows = lambda b, width: slice(b * (PAD + tb) + PAD - (width - 1), b * (PAD + tb) + PAD)
    last_rows = lambda b, width: slice((b + 1) * (PAD + tb) - (width - 1), (b + 1) * (PAD + tb))

    @pl.when(s == 0)
    def _first_tile_early_pieces():
        hb = _normalised_input(xprev_ref, lnpre_ref)
        h_s[...] = hb
        for b in range(bb):
            qc_s[tail_rows(b, CONV_QKV_WIDTH), :] = (
                bufq_ref[0, b] if has_init else jnp.zeros((CONV_QKV_WIDTH - 1, D_QKV), F32))
        for j in EARLY_PIECES:
            _project_piece(bb, tb, j, hb, *proj_bufs)
        for j, followers in EARLY_K_PIECES:
            _project_piece(bb, tb, j, hb, *proj_bufs)
            for kind, *args in followers:
                pieces[kind](*args)
        mix_s[...] = jnp.zeros(mix_s.shape, BF16)

    @pl.when((s % n_t == 0) & (s < n_tiles))
    def _load_stream_state():
        for b in range(bb):
            if has_init:
                ua_s[tail_rows(b, CONV_A_WIDTH), :] = bufa_ref[0, b]
                ns_ref[0, b] = s0_ref[0, b]
            else:
                ua_s[tail_rows(b, CONV_A_WIDTH), :] = jnp.zeros((CONV_A_WIDTH - 1, D_CONV), F32)
                ns_ref[0, b] = jnp.zeros((N_HEADS, HEAD_DIM, HEAD_DIM), F32)

    hb = h_s[...]
    for kind, *args in EARLY_FOLLOWERS:
        pieces[kind](*args)

    def project_pieces(schedule):
        for j, followers in schedule:
            _project_piece(bb, tb, j, hb, *proj_bufs)
            for kind, *args in followers:
                pieces[kind](*args)
            yield

    _interleave([project_pieces(LATE_FIRST)])

    row = lax.broadcasted_iota(jnp.int32, (c, c), 0)
    col = lax.broadcasted_iota(jnp.int32, (c, c), 1)
    incl = jnp.where(row >= col, 0.0, -jnp.inf)
    strict = jnp.where(row > col, 1.0, 0.0)
    eye = jnp.where(row == col, 1.0, 0.0)
    row3 = lax.broadcasted_iota(jnp.int32, (c, 3 * c), 0)
    col3 = lax.broadcasted_iota(jnp.int32, (c, 3 * c), 1) & (c - 1)
    cumsum_mat = jnp.where(row3 >= col3, 1.0, 0.0).astype(BF16)

    pre = {}
    decay_last = {}
    chunks = [(b, n, slice(b * tb + n * c, b * tb + (n + 1) * c))
              for b in range(bb) for n in range(nc)]
    gate_terms = {}
    for b, n, rs in chunks:
        gb = _dot(cumsum_mat, _split3(g_s[rs, :]))
        gl = gb[c - 1:c, :]
        gb_rows = jnp.concatenate([gb, jnp.zeros((LANES - c, LANES), F32)], axis=0).T
        gate_terms[b, n] = (gb, gb_rows, jnp.exp(gb), jnp.exp(gl - gb), jnp.exp(gl), beta_s[rs, :])

    def head_group_chains(start_round, heads):
        for _ in range(start_round):
            yield
        chains = []
        for b, n, rs in chunks:
            gb, gb_rows, e_gb, e_rem, e_gl, beta_c = gate_terms[b, n]
            for hd in heads:
                qh = qkv_s[rs, hd * HEAD_DIM:(hd + 1) * HEAD_DIM]
                kh = qkv_s[rs, D_GDN + hd * HEAD_DIM:D_GDN + (hd + 1) * HEAD_DIM]
                load_v = functools.partial(
                    lambda rows, h: qkv_s[rows, 2 * D_GDN + h * HEAD_DIM:2 * D_GDN + (h + 1) * HEAD_DIM],
                    rs, hd)
                diff = gb[:, hd:hd + 1] - gb_rows[hd:hd + 1, 0:c]
                decay_last[b, n, hd] = e_gl[:, hd:hd + 1]
                chains.append(_chunk_head_precompute(
                    pre, (b, n, hd), c, qh, kh, load_v, beta_c[:, hd:hd + 1], e_gb[:, hd:hd + 1],
                    e_rem[:, hd:hd + 1], diff, incl, strict, eye))
        yield from _round_robin(chains)

    def previous_tile_output():
        mix = mix_s[...]
        this_round = 0
        for j, at_round in enumerate(OUT_PROJ_ROUNDS):
            while this_round < at_round:
                yield
                this_round += 1
            cols = slice(j * PROJ_NBLK, (j + 1) * PROJ_NBLK)
            out_s[:, cols] = _dot(mix, wout_ref[:, cols])
        yield
        out = out_s[...]
        out_var = jnp.mean(out * out, axis=-1, keepdims=True)
        y = (xprev_ref[...].reshape(bb * tb, D_MODEL)
             + out * lax.rsqrt(out_var + RMS_EPS) * lnpost_ref[...])
        y_ref[...] = y.reshape(bb, tb, D_MODEL)

    _interleave([project_pieces(LATE_REST), previous_tile_output()]
                + [head_group_chains(start, heads) for start, heads in HEAD_GROUPS])

    def finish_head(rs, hd):
        def finish(o):
            o_var = jnp.mean(o * o, axis=-1, keepdims=True)
            o = o * lax.rsqrt(o_var + RMS_EPS) * gnw_ref[...]
            mix_s[rs, D_CONV + hd * HEAD_DIM:D_CONV + (hd + 1) * HEAD_DIM] = (
                o * zg_s[rs, hd * HEAD_DIM:(hd + 1) * HEAD_DIM]).astype(BF16)
        return finish

    state = {(b, hd): ns_ref[0, b, hd] for b in range(bb) for hd in range(N_HEADS)}
    mix_s[:, 0:D_CONV] = ya_s[...]

    next_starts_row = (s + 1) % n_t == 0
    for b in range(bb):
        tail_q = qc_s[last_rows(b, CONV_QKV_WIDTH), :]
        tailq_s[b] = tail_q
        fresh = bufqn_ref[0, b] if has_init else jnp.zeros_like(tail_q)
        qc_s[tail_rows(b, CONV_QKV_WIDTH), :] = jnp.where(next_starts_row, fresh, tail_q)

    def recurrence():
        for n in range(nc):
            yield from _round_robin([
                _chunk_head_recurrence(pre[b, n, hd], state, (b, hd), decay_last[b, n, hd],
                                       finish_head(slice(b * tb + n * c, b * tb + (n + 1) * c), hd))
                for b in range(bb) for hd in range(N_HEADS)])

    def next_tile_early_pieces():
        hb_next = _normalised_input(xnext_ref, lnpre_ref)
        h_s[...] = hb_next
        for j, followers in EARLY_K_PIECES:
            _project_piece(bb, tb, j, hb_next, *proj_bufs)
            for kind, *args in followers:
                pieces[kind](*args)
            yield
        for j in EARLY_PIECES:
            _project_piece(bb, tb, j, hb_next, *proj_bufs)
            yield

    _interleave([recurrence(), next_tile_early_pieces()])

    @pl.when(s < n_tiles)
    def _carry_stream_state():
        for (b, hd), value in state.items():
            ns_ref[0, b, hd] = value
        for b in range(bb):
            tail_a = ua_s[last_rows(b, CONV_A_WIDTH), :]
            ua_s[tail_rows(b, CONV_A_WIDTH), :] = tail_a
            na_ref[0, b] = tail_a
            nq_ref[0, b] = tailq_s[b]


def _layer_call(x, init, params, *, bb, tb, c, name):
    bsz, seq, _ = x.shape
    assert bsz % bb == 0 and seq % tb == 0 and tb % c == 0 and tb % SUBLANES == 0 and c <= LANES
    has_init = init is not None
    n_t = seq // tb
    n_tiles = (bsz // bb) * n_t
    rows = bb * tb

    def this_tile(s):
        return jnp.minimum(s, n_tiles - 1)

    def prev_tile(s):
        return jnp.maximum(s - 1, 0)

    def next_tile(s):
        return jnp.minimum(s + 1, n_tiles - 1)

    tile_spec = lambda tile: pl.BlockSpec((bb, tb, D_MODEL), lambda s: (tile(s) // n_t, tile(s) % n_t, 0))
    stream_specs = [
        pl.BlockSpec((1, bb, CONV_A_WIDTH - 1, D_CONV), lambda s: (0, this_tile(s) // n_t, 0, 0)),
        pl.BlockSpec((1, bb, CONV_QKV_WIDTH - 1, D_QKV), lambda s: (0, this_tile(s) // n_t, 0, 0)),
        pl.BlockSpec((1, bb, N_HEADS, HEAD_DIM, HEAD_DIM),
                     lambda s: (0, this_tile(s) // n_t, 0, 0, 0)),
    ]
    in_specs = [tile_spec(prev_tile), tile_spec(next_tile)]
    operands = [x, x]
    if has_init:
        in_specs += stream_specs + [
            pl.BlockSpec((1, bb, CONV_QKV_WIDTH - 1, D_QKV), lambda s: (0, next_tile(s) // n_t, 0, 0))]
        operands += list(init) + [init[1]]
    in_specs += [pl.BlockSpec(p.shape, lambda s, nd=p.ndim: (0,) * nd) for p in params]
    operands += list(params)
    out_shape = (
        jax.ShapeDtypeStruct(x.shape, x.dtype),
        jax.ShapeDtypeStruct((1, bsz, CONV_A_WIDTH - 1, D_CONV), F32),
        jax.ShapeDtypeStruct((1, bsz, CONV_QKV_WIDTH - 1, D_QKV), F32),
        jax.ShapeDtypeStruct((1, bsz, N_HEADS, HEAD_DIM, HEAD_DIM), F32),
    )
    out_specs = [tile_spec(prev_tile)] + stream_specs
    scratch = [
        pltpu.VMEM((rows, D_MODEL), BF16),
        pltpu.VMEM((rows, D_A), F32),
        pltpu.VMEM((bb * (PAD + tb), D_CONV), F32),
        pltpu.VMEM((bb * (PAD + tb), D_QKV), F32),
        pltpu.VMEM((rows, D_GDN), F32),
        pltpu.VMEM((rows, GATE_W), F32),
        pltpu.VMEM((rows, D_QKV), F32),
        pltpu.VMEM((rows, D_GDN), F32),
        pltpu.VMEM((rows, LANES), F32),
        pltpu.VMEM((rows, LANES), F32),
        pltpu.VMEM((rows, D_CONV), BF16),
        pltpu.VMEM((rows, D_MIX), BF16),
        pltpu.VMEM((rows, D_MODEL), F32),
        pltpu.VMEM((bb, CONV_QKV_WIDTH - 1, D_QKV), F32),
    ]
    return pl.pallas_call(
        functools.partial(_layer_kernel, bb=bb, tb=tb, c=c, n_t=n_t, n_tiles=n_tiles,
                          has_init=has_init),
        grid=(n_tiles + 1,),
        in_specs=in_specs,
        out_specs=out_specs,
        out_shape=out_shape,
        scratch_shapes=scratch,
        compiler_params=pltpu.CompilerParams(
            dimension_semantics=("arbitrary",),
            vmem_limit_bytes=VMEM_LIMIT_BYTES),
        name=name,
    )(*operands)


def _tile_shape(bsz, seq):
    tb = min(seq, TILE_ROWS)
    bb = max(1, math.gcd(bsz, TILE_ROWS // tb))
    return bb, tb


def _prepare_params(ln_pre, w_in, conv_a_w, conv_qkv_w, a_log, dt_bias, gdn_norm_w, w_out, ln_post):
    w = w_in.astype(BF16)
    gap = jnp.zeros((D_MODEL, LANES - N_HEADS), BF16)
    w_all = jnp.concatenate([w[:, :D_MAIN + N_HEADS], gap, w[:, D_MAIN + N_HEADS:], gap], axis=1)
    pad_heads = lambda v: jnp.pad(v, (0, LANES - N_HEADS)).reshape(1, LANES)
    return (ln_pre.reshape(1, D_MODEL), w_all, conv_a_w, conv_qkv_w,
            pad_heads(a_log), pad_heads(dt_bias), gdn_norm_w.reshape(1, HEAD_DIM),
            w_out.astype(BF16), ln_post.reshape(1, D_MODEL))


def kernel(x_prompt, x_sample, cache_conv_a, cache_conv_qkv, state_gdn, ln_pre, w_in, conv_a_w,
           conv_qkv_w, a_log, dt_bias, gdn_norm_w, w_out, ln_post):
    assert ln_pre.shape[0] == 1, "single-layer kernel"
    params = _prepare_params(ln_pre[0], w_in[0], conv_a_w[0], conv_qkv_w[0], a_log[0], dt_bias[0],
                             gdn_norm_w[0], w_out[0], ln_post[0])
    bb_p, tb_p = _tile_shape(*x_prompt.shape[:2])
    bb_s, tb_s = _tile_shape(*x_sample.shape[:2])
    y_p, a_p, q_p, s_p = _layer_call(x_prompt, None, params, bb=bb_p, tb=tb_p,
                                     c=math.gcd(x_prompt.shape[1], CHUNK), name="hybrid_prompt")
    y_s, a_s, q_s, s_s = _layer_call(x_sample, (cache_conv_a, cache_conv_qkv, state_gdn), params,
                                     bb=bb_s, tb=tb_s, c=math.gcd(x_sample.shape[1], CHUNK),
                                     name="hybrid_sample")
    return (y_p, y_s, a_p, q_p, s_p, a_s, q_s, s_s)
```

```python
import functools
import math

import jax
import jax.numpy as jnp
from jax import lax
from jax.experimental import pallas as pl
from jax.experimental.pallas import tpu as pltpu

D_MODEL = 1024
D_CONV = 512
CONV_A_WIDTH = 3
N_HEADS = 4
HEAD_DIM = 128
D_GDN = N_HEADS * HEAD_DIM
D_QKV = 3 * D_GDN
CONV_QKV_WIDTH = 4
D_MIX = D_CONV + D_GDN
D_A = 4 * D_CONV
D_MAIN = D_A + D_QKV + D_GDN
CHUNK = 64
RMS_EPS = 1e-6
L2_EPS = 1e-6

LANES = 128
SUBLANES = 8
PAD = SUBLANES
GATE_W = LANES
PROJ_NBLK = 256
TILE_ROWS = 256
VMEM_LIMIT_BYTES = 52 * 1024 * 1024

EARLY_PIECES = (16, 2, 3, 4, 5, 0, 6, 1, 7)
EARLY_FOLLOWERS = (("gates",), ("conv_in",), ("a", 0), ("a", 1), ("a", 2), ("a", 3))
_QKV_PIECE = lambda m: (D_A // PROJ_NBLK + m, (("b", 2 * m), ("b", 2 * m + 1)))
LATE_FIRST = (_QKV_PIECE(0), _QKV_PIECE(2))
LATE_REST = (_QKV_PIECE(1), _QKV_PIECE(3), _QKV_PIECE(4), _QKV_PIECE(5),
             (14, ()), (15, (("silu_zg",),)))
HEAD_GROUPS = ((0, (0, 1)), (2, (2, 3)))
OUT_PROJ_ROUNDS = (0, 0, 3, 3)

F32 = jnp.float32
BF16 = jnp.bfloat16


def _dot(a, b):
    return jnp.dot(a, b, preferred_element_type=F32)


def _dot_nt(a, b):
    return lax.dot_general(a, b, (((1,), (1,)), ((), ())), preferred_element_type=F32)


def _dot_tn(a, b):
    return lax.dot_general(a, b, (((0,), (0,)), ((), ())), preferred_element_type=F32)


def _silu_of_half(h):
    return h + h * jnp.tanh(h)


def _silu(z):
    return _silu_of_half(0.5 * z)


def _split3(x):
    hi = x.astype(BF16)
    r1 = x - hi.astype(F32)
    mid = r1.astype(BF16)
    lo = (r1 - mid.astype(F32)).astype(BF16)
    return jnp.concatenate([hi, mid, lo], axis=0)


def _round_robin(generators):
    alive = list(generators)
    while alive:
        still = []
        for g in alive:
            try:
                next(g)
                still.append(g)
            except StopIteration:
                pass
        alive = still
        yield


def _interleave(generators):
    for _ in _round_robin(generators):
        pass


def _chunk_head_precompute(out, key, c, qh, kh, load_v, bcol, ecol, rcol, diff, incl, strict, eye):
    decay = jnp.exp(diff + incl)
    kb = kh.astype(BF16)
    scores = _dot_nt(jnp.concatenate([kb, qh.astype(BF16)], axis=0), kb)
    qg = (qh * ecol).astype(BF16)
    k_rem = (kh * rcol).astype(BF16)
    k_rhs = kh * (bcol * ecol)
    yield
    a_mat = scores[:c] * (decay * strict) * bcol
    attn = (scores[c:] * decay).astype(BF16)
    levels = int(math.log2(c)) - 1
    ab = a_mat.astype(BF16)
    x = _dot(ab, ab)
    yield
    t = eye - a_mat
    for k in range(1, levels + 1):
        xb = x.astype(BF16)
        if k < levels:
            tx = _dot(jnp.concatenate([t.astype(BF16), xb], axis=0), xb)
            yield
            t = t + tx[:c]
            x = tx[c:]
        else:
            tx = _dot(t.astype(BF16), xb)
            yield
            t = t + tx
    rhs = jnp.concatenate([load_v() * bcol, k_rhs], axis=1)
    uw = _dot(t.astype(BF16), rhs.astype(BF16))
    yield
    out[key] = dict(u=uw[:, :HEAD_DIM], w=uw[:, HEAD_DIM:].astype(BF16), qg=qg, attn=attn,
                    k_rem=k_rem)


def _chunk_head_recurrence(pre, state, key, decay_last, finish):
    s_old = state[key]
    c = pre["u"].shape[0]
    wq_s = _dot(jnp.concatenate([pre["w"], pre["qg"]], axis=0), s_old.astype(BF16))
    yield
    v_new = (pre["u"] - wq_s[:c]).astype(BF16)
    o = wq_s[c:] + _dot(pre["attn"], v_new)
    state[key] = s_old * decay_last + _dot_tn(pre["k_rem"], v_new)
    yield
    finish(o)


def _normalised_input(x_ref, lnpre_ref):
    bb, tb, _ = x_ref.shape
    x = x_ref[...].reshape(bb * tb, D_MODEL)
    var = jnp.mean(x * x, axis=-1, keepdims=True)
    return (x * lax.rsqrt(var + RMS_EPS) * lnpre_ref[...]).astype(BF16)


def _staged_rows(b, tb, back=0):
    first = b * (PAD + tb) + PAD - back
    return slice(first, first + tb)


def _project_piece(bb, tb, j, hb, win_ref, wgate_ref, proj_s, qc_s, zraw_s, gate_s):
    lo = j * PROJ_NBLK
    hi = (j + 1) * PROJ_NBLK
    res = _dot(hb, win_ref[:, lo:hi] if lo < D_MAIN else wgate_ref[...])
    if lo < D_A:
        proj_s[:, lo:hi] = res
    elif lo < D_A + D_QKV:
        for b in range(bb):
            qc_s[_staged_rows(b, tb), lo - D_A:hi - D_A] = res[b * tb:(b + 1) * tb]
    elif lo < D_MAIN:
        zraw_s[:, lo - D_A - D_QKV:hi - D_A - D_QKV] = res
    else:
        gate_s[...] = res


def _elementwise_pieces(bb, tb, cwa_ref, cwq_ref, alog_ref, dtb_ref, proj_s, ua_s, qc_s, zraw_s,
                        gate_s, qkv_s, zg_s, ya_s, g_s, beta_s):
    def causal_conv(buf_s, w_ref, width, cs, halved=False):
        taps = [w_ref[k:k + 1, cs] for k in range(width)]
        if halved:
            taps = [0.5 * w for w in taps]
        rows = []
        for b in range(bb):
            conv = taps[0] * buf_s[_staged_rows(b, tb, width - 1), cs]
            for k in range(1, width):
                conv = conv + taps[k] * buf_s[_staged_rows(b, tb, width - 1 - k), cs]
            rows.append(conv)
        return rows[0] if bb == 1 else jnp.concatenate(rows, axis=0)

    def conv_in():
        for j in range(D_CONV // LANES):
            cs = slice(j * LANES, (j + 1) * LANES)
            u = (proj_s[:, D_CONV + j * LANES:D_CONV + (j + 1) * LANES]
                 * proj_s[:, 2 * D_CONV + j * LANES:2 * D_CONV + (j + 1) * LANES])
            for b in range(bb):
                ua_s[_staged_rows(b, tb), cs] = u[b * tb:(b + 1) * tb]

    def group_a(j):
        cs = slice(j * LANES, (j + 1) * LANES)
        conv = causal_conv(ua_s, cwa_ref, CONV_A_WIDTH, cs)
        z = proj_s[:, 3 * D_CONV + j * LANES:3 * D_CONV + (j + 1) * LANES]
        ya_s[:, cs] = (proj_s[:, cs] * conv * _silu(z)).astype(BF16)

    def group_b(j):
        cs = slice(j * LANES, (j + 1) * LANES)
        a = _silu_of_half(causal_conv(qc_s, cwq_ref, CONV_QKV_WIDTH, cs, halved=True))
        if j < 2 * N_HEADS:
            rn = lax.rsqrt(jnp.sum(a * a, axis=-1, keepdims=True) + L2_EPS)
            if j < N_HEADS:
                rn = rn * (HEAD_DIM ** -0.5)
            a = a * rn
        qkv_s[:, cs] = a

    def gates():
        gate = gate_s[:, 0:LANES]
        xs = gate + dtb_ref[...]
        softplus = jnp.maximum(xs, 0.0) + jnp.log1p(jnp.exp(-jnp.abs(xs)))
        g_s[...] = -jnp.exp(alog_ref[...]) * softplus
        beta_s[...] = 1.0 / (1.0 + jnp.exp(-gate))

    def silu_zg():
        zg_s[...] = _silu(zraw_s[...])

    return {"gates": gates, "conv_in": conv_in, "a": group_a, "b": group_b, "silu_zg": silu_zg}


def _layer_kernel(*refs, bb, tb, c, n_t, n_tiles, has_init):
    refs = list(refs)
    xprev_ref, xnext_ref = refs.pop(0), refs.pop(0)
    if has_init:
        bufa_ref, bufq_ref, s0_ref = refs.pop(0), refs.pop(0), refs.pop(0)
    (lnpre_ref, win_ref, wgate_ref, cwa_ref, cwq_ref, alog_ref, dtb_ref, gnw_ref, wout_ref, lnpost_ref,
     y_ref, na_ref, nq_ref, ns_ref,
     h_s, proj_s, ua_s, qc_s, zraw_s, gate_s, qkv_s, zg_s, g_s, beta_s, ya_s, mix_s, out_s) = refs
    nc = tb // c
    s = pl.program_id(0)
    proj_bufs = (win_ref, wgate_ref, proj_s, qc_s, zraw_s, gate_s)
    pieces = _elementwise_pieces(bb, tb, cwa_ref, cwq_ref, alog_ref, dtb_ref, proj_s, ua_s, qc_s,
                                 zraw_s, gate_s, qkv_s, zg_s, ya_s, g_s, beta_s)
    tail_rows = lambda b, width: slice(b * (PAD + tb) + PAD - (width - 1), b * (PAD + tb) + PAD)
    last_rows = lambda b, width: slice((b + 1) * (PAD + tb) - (width - 1), (b + 1) * (PAD + tb))

    @pl.when(s == 0)
    def _first_tile_early_pieces():
        hb = _normalised_input(xprev_ref, lnpre_ref)
        h_s[...] = hb
        for j in EARLY_PIECES:
            _project_piece(bb, tb, j, hb, *proj_bufs)
        mix_s[...] = jnp.zeros(mix_s.shape, BF16)

    @pl.when((s % n_t == 0) & (s < n_tiles))
    def _load_stream_state():
        for b in range(bb):
            if has_init:
                ua_s[tail_rows(b, CONV_A_WIDTH), :] = bufa_ref[0, b]
                qc_s[tail_rows(b, CONV_QKV_WIDTH), :] = bufq_ref[0, b]
                ns_ref[0, b] = s0_ref[0, b]
            else:
                ua_s[tail_rows(b, CONV_A_WIDTH), :] = jnp.zeros((CONV_A_WIDTH - 1, D_CONV), F32)
                qc_s[tail_rows(b, CONV_QKV_WIDTH), :] = jnp.zeros((CONV_QKV_WIDTH - 1, D_QKV), F32)
                ns_ref[0, b] = jnp.zeros((N_HEADS, HEAD_DIM, HEAD_DIM), F32)

    hb = h_s[...]
    for kind, *args in EARLY_FOLLOWERS:
        pieces[kind](*args)

    def project_pieces(schedule):
        for j, followers in schedule:
            _project_piece(bb, tb, j, hb, *proj_bufs)
            for kind, *args in followers:
                pieces[kind](*args)
            yield

    _interleave([project_pieces(LATE_FIRST)])

    row = lax.broadcasted_iota(jnp.int32, (c, c), 0)
    col = lax.broadcasted_iota(jnp.int32, (c, c), 1)
    incl = jnp.where(row >= col, 0.0, -jnp.inf)
    strict = jnp.where(row > col, 1.0, 0.0)
    eye = jnp.where(row == col, 1.0, 0.0)
    row3 = lax.broadcasted_iota(jnp.int32, (c, 3 * c), 0)
    col3 = lax.broadcasted_iota(jnp.int32, (c, 3 * c), 1) & (c - 1)
    cumsum_mat = jnp.where(row3 >= col3, 1.0, 0.0).astype(BF16)

    pre = {}
    decay_last = {}
    chunks = [(b, n, slice(b * tb + n * c, b * tb + (n + 1) * c))
              for b in range(bb) for n in range(nc)]
    gate_terms = {}
    for b, n, rs in chunks:
        gb = _dot(cumsum_mat, _split3(g_s[rs, :]))
        gl = gb[c - 1:c, :]
        gb_rows = jnp.concatenate([gb, jnp.zeros((LANES - c, LANES), F32)], axis=0).T
        gate_terms[b, n] = (gb, gb_rows, jnp.exp(gb), jnp.exp(gl - gb), jnp.exp(gl), beta_s[rs, :])

    def head_group_chains(start_round, heads):
        for _ in range(start_round):
            yield
        chains = []
        for b, n, rs in chunks:
            gb, gb_rows, e_gb, e_rem, e_gl, beta_c = gate_terms[b, n]
            for hd in heads:
                qh = qkv_s[rs, hd * HEAD_DIM:(hd + 1) * HEAD_DIM]
                kh = qkv_s[rs, D_GDN + hd * HEAD_DIM:D_GDN + (hd + 1) * HEAD_DIM]
                load_v = functools.partial(
                    lambda rows, h: qkv_s[rows, 2 * D_GDN + h * HEAD_DIM:2 * D_GDN + (h + 1) * HEAD_DIM],
                    rs, hd)
                diff = gb[:, hd:hd + 1] - gb_rows[hd:hd + 1, 0:c]
                decay_last[b, n, hd] = e_gl[:, hd:hd + 1]
                chains.append(_chunk_head_precompute(
                    pre, (b, n, hd), c, qh, kh, load_v,
                    beta_c[:, N_HEADS + hd:N_HEADS + hd + 1], e_gb[:, hd:hd + 1],
                    e_rem[:, hd:hd + 1], diff, incl, strict, eye))
        yield from _round_robin(chains)

    def previous_tile_output():
        mix = mix_s[...]
        this_round = 0
        for j, at_round in enumerate(OUT_PROJ_ROUNDS):
            while this_round < at_round:
                yield
                this_round += 1
            cols = slice(j * PROJ_NBLK, (j + 1) * PROJ_NBLK)
            out_s[:, cols] = _dot(mix, wout_ref[:, cols])
        yield
        out = out_s[...]
        out_var = jnp.mean(out * out, axis=-1, keepdims=True)
        y = (xprev_ref[...].reshape(bb * tb, D_MODEL)
             + out * lax.rsqrt(out_var + RMS_EPS) * lnpost_ref[...])
        y_ref[...] = y.reshape(bb, tb, D_MODEL)

    _interleave([project_pieces(LATE_REST), previous_tile_output()]
                + [head_group_chains(start, heads) for start, heads in HEAD_GROUPS])

    def finish_head(rs, hd):
        def finish(o):
            o_var = jnp.mean(o * o, axis=-1, keepdims=True)
            o = o * lax.rsqrt(o_var + RMS_EPS) * gnw_ref[...]
            mix_s[rs, D_CONV + hd * HEAD_DIM:D_CONV + (hd + 1) * HEAD_DIM] = (
                o * zg_s[rs, hd * HEAD_DIM:(hd + 1) * HEAD_DIM]).astype(BF16)
        return finish

    state = {(b, hd): ns_ref[0, b, hd] for b in range(bb) for hd in range(N_HEADS)}
    mix_s[:, 0:D_CONV] = ya_s[...]

    def recurrence():
        for n in range(nc):
            yield from _round_robin([
                _chunk_head_recurrence(pre[b, n, hd], state, (b, hd), decay_last[b, n, hd],
                                       finish_head(slice(b * tb + n * c, b * tb + (n + 1) * c), hd))
                for b in range(bb) for hd in range(N_HEADS)])

    def next_tile_early_pieces():
        hb_next = _normalised_input(xnext_ref, lnpre_ref)
        h_s[...] = hb_next
        for j in EARLY_PIECES:
            _project_piece(bb, tb, j, hb_next, *proj_bufs)
            yield

    _interleave([recurrence(), next_tile_early_pieces()])

    @pl.when(s < n_tiles)
    def _carry_stream_state():
        for (b, hd), value in state.items():
            ns_ref[0, b, hd] = value
        for b in range(bb):
            tail_a = ua_s[last_rows(b, CONV_A_WIDTH), :]
            ua_s[tail_rows(b, CONV_A_WIDTH), :] = tail_a
            na_ref[0, b] = tail_a
            tail_q = qc_s[last_rows(b, CONV_QKV_WIDTH), :]
            qc_s[tail_rows(b, CONV_QKV_WIDTH), :] = tail_q
            nq_ref[0, b] = tail_q


def _layer_call(x, init, params, *, bb, tb, c, name):
    bsz, seq, _ = x.shape
    assert bsz % bb == 0 and seq % tb == 0 and tb % c == 0 and tb % SUBLANES == 0 and c <= LANES
    has_init = init is not None
    n_t = seq // tb
    n_tiles = (bsz // bb) * n_t
    rows = bb * tb

    def this_tile(s):
        return jnp.minimum(s, n_tiles - 1)

    def prev_tile(s):
        return jnp.maximum(s - 1, 0)

    def next_tile(s):
        return jnp.minimum(s + 1, n_tiles - 1)

    tile_spec = lambda tile: pl.BlockSpec((bb, tb, D_MODEL), lambda s: (tile(s) // n_t, tile(s) % n_t, 0))
    stream_specs = [
        pl.BlockSpec((1, bb, CONV_A_WIDTH - 1, D_CONV), lambda s: (0, this_tile(s) // n_t, 0, 0)),
        pl.BlockSpec((1, bb, CONV_QKV_WIDTH - 1, D_QKV), lambda s: (0, this_tile(s) // n_t, 0, 0)),
        pl.BlockSpec((1, bb, N_HEADS, HEAD_DIM, HEAD_DIM),
                     lambda s: (0, this_tile(s) // n_t, 0, 0, 0)),
    ]
    in_specs = [tile_spec(prev_tile), tile_spec(next_tile)]
    operands = [x, x]
    if has_init:
        in_specs += stream_specs
        operands += list(init)
    in_specs += [pl.BlockSpec(p.shape, lambda s, nd=p.ndim: (0,) * nd) for p in params]
    operands += list(params)
    out_shape = (
        jax.ShapeDtypeStruct(x.shape, x.dtype),
        jax.ShapeDtypeStruct((1, bsz, CONV_A_WIDTH - 1, D_CONV), F32),
        jax.ShapeDtypeStruct((1, bsz, CONV_QKV_WIDTH - 1, D_QKV), F32),
        jax.ShapeDtypeStruct((1, bsz, N_HEADS, HEAD_DIM, HEAD_DIM), F32),
    )
    out_specs = [tile_spec(prev_tile)] + stream_specs
    scratch = [
        pltpu.VMEM((rows, D_MODEL), BF16),
        pltpu.VMEM((rows, D_A), F32),
        pltpu.VMEM((bb * (PAD + tb), D_CONV), F32),
        pltpu.VMEM((bb * (PAD + tb), D_QKV), F32),
        pltpu.VMEM((rows, D_GDN), F32),
        pltpu.VMEM((rows, GATE_W), F32),
        pltpu.VMEM((rows, D_QKV), F32),
        pltpu.VMEM((rows, D_GDN), F32),
        pltpu.VMEM((rows, LANES), F32),
        pltpu.VMEM((rows, LANES), F32),
        pltpu.VMEM((rows, D_CONV), BF16),
        pltpu.VMEM((rows, D_MIX), BF16),
        pltpu.VMEM((rows, D_MODEL), F32),
    ]
    return pl.pallas_call(
        functools.partial(_layer_kernel, bb=bb, tb=tb, c=c, n_t=n_t, n_tiles=n_tiles,
                          has_init=has_init),
        grid=(n_tiles + 1,),
        in_specs=in_specs,
        out_specs=out_specs,
        out_shape=out_shape,
        scratch_shapes=scratch,
        compiler_params=pltpu.CompilerParams(
            dimension_semantics=("arbitrary",),
            vmem_limit_bytes=VMEM_LIMIT_BYTES),
        name=name,
    )(*operands)


def _tile_shape(bsz, seq):
    tb = min(seq, TILE_ROWS)
    bb = max(1, math.gcd(bsz, TILE_ROWS // tb))
    return bb, tb


def _prepare_params(ln_pre, w_in, conv_a_w, conv_qkv_w, a_log, dt_bias, gdn_norm_w, w_out, ln_post):
    w = w_in.astype(BF16)
    w_gate = jnp.pad(w[:, D_MAIN:], ((0, 0), (0, GATE_W - 2 * N_HEADS)))
    pad_heads = lambda v: jnp.pad(v, (0, LANES - N_HEADS)).reshape(1, LANES)
    return (ln_pre.reshape(1, D_MODEL), w, w_gate, conv_a_w, conv_qkv_w,
            pad_heads(a_log), pad_heads(dt_bias), gdn_norm_w.reshape(1, HEAD_DIM),
            w_out.astype(BF16), ln_post.reshape(1, D_MODEL))


def kernel(x_prompt, x_sample, cache_conv_a, cache_conv_qkv, state_gdn, ln_pre, w_in, conv_a_w,
           conv_qkv_w, a_log, dt_bias, gdn_norm_w, w_out, ln_post):
    assert ln_pre.shape[0] == 1, "single-layer kernel"
    params = _prepare_params(ln_pre[0], w_in[0], conv_a_w[0], conv_qkv_w[0], a_log[0], dt_bias[0],
                             gdn_norm_w[0], w_out[0], ln_post[0])
    bb_p, tb_p = _tile_shape(*x_prompt.shape[:2])
    bb_s, tb_s = _tile_shape(*x_sample.shape[:2])
    y_p, a_p, q_p, s_p = _layer_call(x_prompt, None, params, bb=bb_p, tb=tb_p,
                                     c=math.gcd(x_prompt.shape[1], CHUNK), name="hybrid_prompt")
    y_s, a_s, q_s, s_s = _layer_call(x_sample, (cache_conv_a, cache_conv_qkv, state_gdn), params,
                                     bb=bb_s, tb=tb_s, c=math.gcd(x_sample.shape[1], CHUNK),
                                     name="hybrid_sample")
    return (y_p, y_s, a_p, q_p, s_p, a_s, q_s, s_s)
```

```python
import functools
import math

import jax
import jax.numpy as jnp
from jax import lax
from jax.experimental import pallas as pl
from jax.experimental.pallas import tpu as pltpu

D_MODEL = 1024
D_CONV = 512
CONV_A_WIDTH = 3
N_HEADS = 4
HEAD_DIM = 128
D_GDN = N_HEADS * HEAD_DIM
D_QKV = 3 * D_GDN
CONV_QKV_WIDTH = 4
D_MIX = D_CONV + D_GDN
D_A = 4 * D_CONV
D_MAIN = D_A + D_QKV + D_GDN
CHUNK = 64
RMS_EPS = 1e-6
L2_EPS = 1e-6

LANES = 128
SUBLANES = 8
PAD = SUBLANES
GATE_W = 2 * LANES
PROJ_NBLK = 256
TILE_ROWS = 256
VMEM_LIMIT_BYTES = 52 * 1024 * 1024

EARLY_PIECES = (16, 2, 3, 4, 5, 0, 6, 1, 7)
EARLY_FOLLOWERS = (("gates",), ("conv_in",), ("a", 0), ("a", 1), ("a", 2), ("a", 3))
_QKV_PIECE = lambda m: (D_A // PROJ_NBLK + m, (("b", 2 * m), ("b", 2 * m + 1)))
LATE_FIRST = (_QKV_PIECE(0), _QKV_PIECE(2))
LATE_REST = (_QKV_PIECE(1), _QKV_PIECE(3), _QKV_PIECE(4), _QKV_PIECE(5),
             (14, ()), (15, (("silu_zg",),)))
HEAD_GROUPS = ((0, (0, 1)), (2, (2, 3)))
OUT_PROJ_ROUNDS = (0, 0, 3, 3)

F32 = jnp.float32
BF16 = jnp.bfloat16


def _dot(a, b):
    return jnp.dot(a, b, preferred_element_type=F32)


def _dot_nt(a, b):
    return lax.dot_general(a, b, (((1,), (1,)), ((), ())), preferred_element_type=F32)


def _dot_tn(a, b):
    return lax.dot_general(a, b, (((0,), (0,)), ((), ())), preferred_element_type=F32)


def _silu_of_half(h):
    return h + h * jnp.tanh(h)


def _silu(z):
    return _silu_of_half(0.5 * z)


def _split3(x):
    hi = x.astype(BF16)
    r1 = x - hi.astype(F32)
    mid = r1.astype(BF16)
    lo = (r1 - mid.astype(F32)).astype(BF16)
    return jnp.concatenate([hi, mid, lo], axis=0)


def _round_robin(generators):
    alive = list(generators)
    while alive:
        still = []
        for g in alive:
            try:
                next(g)
                still.append(g)
            except StopIteration:
                pass
        alive = still
        yield


def _interleave(generators):
    for _ in _round_robin(generators):
        pass


def _chunk_head_precompute(out, key, c, qh, kh, load_v, bcol, ecol, rcol, diff, incl, strict, eye):
    decay = jnp.exp(diff + incl)
    kb = kh.astype(BF16)
    scores = _dot_nt(jnp.concatenate([kb, qh.astype(BF16)], axis=0), kb)
    qg = (qh * ecol).astype(BF16)
    k_rem = (kh * rcol).astype(BF16)
    k_rhs = kh * (bcol * ecol)
    yield
    a_mat = scores[:c] * (decay * strict) * bcol
    attn = (scores[c:] * decay).astype(BF16)
    levels = int(math.log2(c)) - 1
    ab = a_mat.astype(BF16)
    x = _dot(ab, ab)
    yield
    t = eye - a_mat
    for k in range(1, levels + 1):
        xb = x.astype(BF16)
        if k < levels:
            tx = _dot(jnp.concatenate([t.astype(BF16), xb], axis=0), xb)
            yield
            t = t + tx[:c]
            x = tx[c:]
        else:
            tx = _dot(t.astype(BF16), xb)
            yield
            t = t + tx
    rhs = jnp.concatenate([load_v() * bcol, k_rhs], axis=1)
    uw = _dot(t.astype(BF16), rhs.astype(BF16))
    yield
    out[key] = dict(u=uw[:, :HEAD_DIM], w=uw[:, HEAD_DIM:].astype(BF16), qg=qg, attn=attn,
                    k_rem=k_rem)


def _chunk_head_recurrence(pre, state, key, decay_last, finish):
    s_old = state[key]
    c = pre["u"].shape[0]
    wq_s = _dot(jnp.concatenate([pre["w"], pre["qg"]], axis=0), s_old.astype(BF16))
    yield
    v_new = (pre["u"] - wq_s[:c]).astype(BF16)
    o = wq_s[c:] + _dot(pre["attn"], v_new)
    state[key] = s_old * decay_last + _dot_tn(pre["k_rem"], v_new)
    yield
    finish(o)


def _normalised_input(x_ref, lnpre_ref):
    bb, tb, _ = x_ref.shape
    x = x_ref[...].reshape(bb * tb, D_MODEL)
    var = jnp.mean(x * x, axis=-1, keepdims=True)
    return (x * lax.rsqrt(var + RMS_EPS) * lnpre_ref[...]).astype(BF16)


def _staged_rows(b, tb, back=0):
    first = b * (PAD + tb) + PAD - back
    return slice(first, first + tb)


def _project_piece(bb, tb, j, hb, win_ref, wgate_ref, proj_s, qc_s, zraw_s, gate_s):
    lo = j * PROJ_NBLK
    hi = (j + 1) * PROJ_NBLK
    res = _dot(hb, win_ref[:, lo:hi] if lo < D_MAIN else wgate_ref[...])
    if lo < D_A:
        proj_s[:, lo:hi] = res
    elif lo < D_A + D_QKV:
        for b in range(bb):
            qc_s[_staged_rows(b, tb), lo - D_A:hi - D_A] = res[b * tb:(b + 1) * tb]
    elif lo < D_MAIN:
        zraw_s[:, lo - D_A - D_QKV:hi - D_A - D_QKV] = res
    else:
        gate_s[...] = res


def _elementwise_pieces(bb, tb, cwa_ref, cwq_ref, alog_ref, dtb_ref, proj_s, ua_s, qc_s, zraw_s,
                        gate_s, qkv_s, zg_s, ya_s, g_s, beta_s):
    def causal_conv(buf_s, w_ref, width, cs, halved=False):
        taps = [w_ref[k:k + 1, cs] for k in range(width)]
        if halved:
            taps = [0.5 * w for w in taps]
        rows = []
        for b in range(bb):
            conv = taps[0] * buf_s[_staged_rows(b, tb, width - 1), cs]
            for k in range(1, width):
                conv = conv + taps[k] * buf_s[_staged_rows(b, tb, width - 1 - k), cs]
            rows.append(conv)
        return rows[0] if bb == 1 else jnp.concatenate(rows, axis=0)

    def conv_in():
        for j in range(D_CONV // LANES):
            cs = slice(j * LANES, (j + 1) * LANES)
            u = (proj_s[:, D_CONV + j * LANES:D_CONV + (j + 1) * LANES]
                 * proj_s[:, 2 * D_CONV + j * LANES:2 * D_CONV + (j + 1) * LANES])
            for b in range(bb):
                ua_s[_staged_rows(b, tb), cs] = u[b * tb:(b + 1) * tb]

    def group_a(j):
        cs = slice(j * LANES, (j + 1) * LANES)
        conv = causal_conv(ua_s, cwa_ref, CONV_A_WIDTH, cs)
        z = proj_s[:, 3 * D_CONV + j * LANES:3 * D_CONV + (j + 1) * LANES]
        ya_s[:, cs] = (proj_s[:, cs] * conv * _silu(z)).astype(BF16)

    def group_b(j):
        cs = slice(j * LANES, (j + 1) * LANES)
        a = _silu_of_half(causal_conv(qc_s, cwq_ref, CONV_QKV_WIDTH, cs, halved=True))
        if j < 2 * N_HEADS:
            rn = lax.rsqrt(jnp.sum(a * a, axis=-1, keepdims=True) + L2_EPS)
            if j < N_HEADS:
                rn = rn * (HEAD_DIM ** -0.5)
            a = a * rn
        qkv_s[:, cs] = a

    def gates():
        xs = gate_s[:, 0:LANES] + dtb_ref[...]
        softplus = jnp.maximum(xs, 0.0) + jnp.log1p(jnp.exp(-jnp.abs(xs)))
        g_s[...] = -jnp.exp(alog_ref[...]) * softplus
        beta_s[...] = 1.0 / (1.0 + jnp.exp(-gate_s[:, LANES:2 * LANES]))

    def silu_zg():
        zg_s[...] = _silu(zraw_s[...])

    return {"gates": gates, "conv_in": conv_in, "a": group_a, "b": group_b, "silu_zg": silu_zg}


def _layer_kernel(*refs, bb, tb, c, n_t, n_tiles, has_init):
    refs = list(refs)
    xprev_ref, xnext_ref = refs.pop(0), refs.pop(0)
    if has_init:
        bufa_ref, bufq_ref, s0_ref = refs.pop(0), refs.pop(0), refs.pop(0)
    (lnpre_ref, win_ref, wgate_ref, cwa_ref, cwq_ref, alog_ref, dtb_ref, gnw_ref, wout_ref, lnpost_ref,
     y_ref, na_ref, nq_ref, ns_ref,
     h_s, proj_s, ua_s, qc_s, zraw_s, gate_s, qkv_s, zg_s, g_s, beta_s, ya_s, mix_s, out_s) = refs
    nc = tb // c
    s = pl.program_id(0)
    proj_bufs = (win_ref, wgate_ref, proj_s, qc_s, zraw_s, gate_s)
    pieces = _elementwise_pieces(bb, tb, cwa_ref, cwq_ref, alog_ref, dtb_ref, proj_s, ua_s, qc_s,
                                 zraw_s, gate_s, qkv_s, zg_s, ya_s, g_s, beta_s)
    tail_rows = lambda b, width: slice(b * (PAD + tb) + PAD - (width - 1), b * (PAD + tb) + PAD)
    last_rows = lambda b, width: slice((b + 1) * (PAD + tb) - (width - 1), (b + 1) * (PAD + tb))

    @pl.when(s == 0)
    def _first_tile_early_pieces():
        hb = _normalised_input(xprev_ref, lnpre_ref)
        h_s[...] = hb
        for j in EARLY_PIECES:
            _project_piece(bb, tb, j, hb, *proj_bufs)
        mix_s[...] = jnp.zeros(mix_s.shape, BF16)

    @pl.when((s % n_t == 0) & (s < n_tiles))
    def _load_stream_state():
        for b in range(bb):
            if has_init:
                ua_s[tail_rows(b, CONV_A_WIDTH), :] = bufa_ref[0, b]
                qc_s[tail_rows(b, CONV_QKV_WIDTH), :] = bufq_ref[0, b]
                ns_ref[0, b] = s0_ref[0, b]
            else:
                ua_s[tail_rows(b, CONV_A_WIDTH), :] = jnp.zeros((CONV_A_WIDTH - 1, D_CONV), F32)
                qc_s[tail_rows(b, CONV_QKV_WIDTH), :] = jnp.zeros((CONV_QKV_WIDTH - 1, D_QKV), F32)
                ns_ref[0, b] = jnp.zeros((N_HEADS, HEAD_DIM, HEAD_DIM), F32)

    hb = h_s[...]
    for kind, *args in EARLY_FOLLOWERS:
        pieces[kind](*args)

    def project_pieces(schedule):
        for j, followers in schedule:
            _project_piece(bb, tb, j, hb, *proj_bufs)
            for kind, *args in followers:
                pieces[kind](*args)
            yield

    _interleave([project_pieces(LATE_FIRST)])

    row = lax.broadcasted_iota(jnp.int32, (c, c), 0)
    col = lax.broadcasted_iota(jnp.int32, (c, c), 1)
    incl = jnp.where(row >= col, 0.0, -jnp.inf)
    strict = jnp.where(row > col, 1.0, 0.0)
    eye = jnp.where(row == col, 1.0, 0.0)
    row3 = lax.broadcasted_iota(jnp.int32, (c, 3 * c), 0)
    col3 = lax.broadcasted_iota(jnp.int32, (c, 3 * c), 1) & (c - 1)
    cumsum_mat = jnp.where(row3 >= col3, 1.0, 0.0).astype(BF16)

    pre = {}
    decay_last = {}
    chunks = [(b, n, slice(b * tb + n * c, b * tb + (n + 1) * c))
              for b in range(bb) for n in range(nc)]
    gate_terms = {}
    for b, n, rs in chunks:
        gb = _dot(cumsum_mat, _split3(g_s[rs, :]))
        gl = gb[c - 1:c, :]
        gb_rows = jnp.concatenate([gb, jnp.zeros((LANES - c, LANES), F32)], axis=0).T
        gate_terms[b, n] = (gb, gb_rows, jnp.exp(gb), jnp.exp(gl - gb), jnp.exp(gl), beta_s[rs, :])

    def head_group_chains(start_round, heads):
        for _ in range(start_round):
            yield
        chains = []
        for b, n, rs in chunks:
            gb, gb_rows, e_gb, e_rem, e_gl, beta_c = gate_terms[b, n]
            for hd in heads:
                qh = qkv_s[rs, hd * HEAD_DIM:(hd + 1) * HEAD_DIM]
                kh = qkv_s[rs, D_GDN + hd * HEAD_DIM:D_GDN + (hd + 1) * HEAD_DIM]
                load_v = functools.partial(
                    lambda rows, h: qkv_s[rows, 2 * D_GDN + h * HEAD_DIM:2 * D_GDN + (h + 1) * HEAD_DIM],
                    rs, hd)
                diff = gb[:, hd:hd + 1] - gb_rows[hd:hd + 1, 0:c]
                decay_last[b, n, hd] = e_gl[:, hd:hd + 1]
                chains.append(_chunk_head_precompute(
                    pre, (b, n, hd), c, qh, kh, load_v, beta_c[:, hd:hd + 1], e_gb[:, hd:hd + 1],
                    e_rem[:, hd:hd + 1], diff, incl, strict, eye))
        yield from _round_robin(chains)

    def previous_tile_output():
        mix = mix_s[...]
        this_round = 0
        for j, at_round in enumerate(OUT_PROJ_ROUNDS):
            while this_round < at_round:
                yield
                this_round += 1
            cols = slice(j * PROJ_NBLK, (j + 1) * PROJ_NBLK)
            out_s[:, cols] = _dot(mix, wout_ref[:, cols])
        yield
        out = out_s[...]
        out_var = jnp.mean(out * out, axis=-1, keepdims=True)
        y = (xprev_ref[...].reshape(bb * tb, D_MODEL)
             + out * lax.rsqrt(out_var + RMS_EPS) * lnpost_ref[...])
        y_ref[...] = y.reshape(bb, tb, D_MODEL)

    _interleave([project_pieces(LATE_REST), previous_tile_output()]
                + [head_group_chains(start, heads) for start, heads in HEAD_GROUPS])

    def finish_head(rs, hd):
        def finish(o):
            o_var = jnp.mean(o * o, axis=-1, keepdims=True)
            o = o * lax.rsqrt(o_var + RMS_EPS) * gnw_ref[...]
            mix_s[rs, D_CONV + hd * HEAD_DIM:D_CONV + (hd + 1) * HEAD_DIM] = (
                o * zg_s[rs, hd * HEAD_DIM:(hd + 1) * HEAD_DIM]).astype(BF16)
        return finish

    state = {(b, hd): ns_ref[0, b, hd] for b in range(bb) for hd in range(N_HEADS)}
    mix_s[:, 0:D_CONV] = ya_s[...]

    def recurrence():
        for n in range(nc):
            yield from _round_robin([
                _chunk_head_recurrence(pre[b, n, hd], state, (b, hd), decay_last[b, n, hd],
                                       finish_head(slice(b * tb + n * c, b * tb + (n + 1) * c), hd))
                for b in range(bb) for hd in range(N_HEADS)])

    def next_tile_early_pieces():
        hb_next = _normalised_input(xnext_ref, lnpre_ref)
        h_s[...] = hb_next
        for j in EARLY_PIECES:
            _project_piece(bb, tb, j, hb_next, *proj_bufs)
            yield

    _interleave([recurrence(), next_tile_early_pieces()])

    @pl.when(s < n_tiles)
    def _carry_stream_state():
        for (b, hd), value in state.items():
            ns_ref[0, b, hd] = value
        for b in range(bb):
            tail_a = ua_s[last_rows(b, CONV_A_WIDTH), :]
            ua_s[tail_rows(b, CONV_A_WIDTH), :] = tail_a
            na_ref[0, b] = tail_a
            tail_q = qc_s[last_rows(b, CONV_QKV_WIDTH), :]
            qc_s[tail_rows(b, CONV_QKV_WIDTH), :] = tail_q
            nq_ref[0, b] = tail_q


def _layer_call(x, init, params, *, bb, tb, c, name):
    bsz, seq, _ = x.shape
    assert bsz % bb == 0 and seq % tb == 0 and tb % c == 0 and tb % SUBLANES == 0 and c <= LANES
    has_init = init is not None
    n_t = seq // tb
    n_tiles = (bsz // bb) * n_t
    rows = bb * tb

    def this_tile(s):
        return jnp.minimum(s, n_tiles - 1)

    def prev_tile(s):
        return jnp.maximum(s - 1, 0)

    def next_tile(s):
        return jnp.minimum(s + 1, n_tiles - 1)

    tile_spec = lambda tile: pl.BlockSpec((bb, tb, D_MODEL), lambda s: (tile(s) // n_t, tile(s) % n_t, 0))
    stream_specs = [
        pl.BlockSpec((1, bb, CONV_A_WIDTH - 1, D_CONV), lambda s: (0, this_tile(s) // n_t, 0, 0)),
        pl.BlockSpec((1, bb, CONV_QKV_WIDTH - 1, D_QKV), lambda s: (0, this_tile(s) // n_t, 0, 0)),
        pl.BlockSpec((1, bb, N_HEADS, HEAD_DIM, HEAD_DIM),
                     lambda s: (0, this_tile(s) // n_t, 0, 0, 0)),
    ]
    in_specs = [tile_spec(prev_tile), tile_spec(next_tile)]
    operands = [x, x]
    if has_init:
        in_specs += stream_specs
        operands += list(init)
    in_specs += [pl.BlockSpec(p.shape, lambda s, nd=p.ndim: (0,) * nd) for p in params]
    operands += list(params)
    out_shape = (
        jax.ShapeDtypeStruct(x.shape, x.dtype),
        jax.ShapeDtypeStruct((1, bsz, CONV_A_WIDTH - 1, D_CONV), F32),
        jax.ShapeDtypeStruct((1, bsz, CONV_QKV_WIDTH - 1, D_QKV), F32),
        jax.ShapeDtypeStruct((1, bsz, N_HEADS, HEAD_DIM, HEAD_DIM), F32),
    )
    out_specs = [tile_spec(prev_tile)] + stream_specs
    scratch = [
        pltpu.VMEM((rows, D_MODEL), BF16),
        pltpu.VMEM((rows, D_A), F32),
        pltpu.VMEM((bb * (PAD + tb), D_CONV), F32),
        pltpu.VMEM((bb * (PAD + tb), D_QKV), F32),
        pltpu.VMEM((rows, D_GDN), F32),
        pltpu.VMEM((rows, GATE_W), F32),
        pltpu.VMEM((rows, D_QKV), F32),
        pltpu.VMEM((rows, D_GDN), F32),
        pltpu.VMEM((rows, LANES), F32),
        pltpu.VMEM((rows, LANES), F32),
        pltpu.VMEM((rows, D_CONV), BF16),
        pltpu.VMEM((rows, D_MIX), BF16),
        pltpu.VMEM((rows, D_MODEL), F32),
    ]
    return pl.pallas_call(
        functools.partial(_layer_kernel, bb=bb, tb=tb, c=c, n_t=n_t, n_tiles=n_tiles,
                          has_init=has_init),
        grid=(n_tiles + 1,),
        in_specs=in_specs,
        out_specs=out_specs,
        out_shape=out_shape,
        scratch_shapes=scratch,
        compiler_params=pltpu.CompilerParams(
            dimension_semantics=("arbitrary",),
            vmem_limit_bytes=VMEM_LIMIT_BYTES),
        name=name,
    )(*operands)


def _tile_shape(bsz, seq):
    tb = min(seq, TILE_ROWS)
    bb = max(1, math.gcd(bsz, TILE_ROWS // tb))
    return bb, tb


def _prepare_params(ln_pre, w_in, conv_a_w, conv_qkv_w, a_log, dt_bias, gdn_norm_w, w_out, ln_post):
    w = w_in.astype(BF16)
    gap = jnp.zeros((D_MODEL, LANES - N_HEADS), BF16)
    w_gate = jnp.concatenate([w[:, D_MAIN:D_MAIN + N_HEADS], gap, w[:, D_MAIN + N_HEADS:], gap], axis=1)
    pad_heads = lambda v: jnp.pad(v, (0, LANES - N_HEADS)).reshape(1, LANES)
    return (ln_pre.reshape(1, D_MODEL), w, w_gate, conv_a_w, conv_qkv_w,
            pad_heads(a_log), pad_heads(dt_bias), gdn_norm_w.reshape(1, HEAD_DIM),
            w_out.astype(BF16), ln_post.reshape(1, D_MODEL))


def kernel(x_prompt, x_sample, cache_conv_a, cache_conv_qkv, state_gdn, ln_pre, w_in, conv_a_w,
           conv_qkv_w, a_log, dt_bias, gdn_norm_w, w_out, ln_post):
    assert ln_pre.shape[0] == 1, "single-layer kernel"
    params = _prepare_params(ln_pre[0], w_in[0], conv_a_w[0], conv_qkv_w[0], a_log[0], dt_bias[0],
                             gdn_norm_w[0], w_out[0], ln_post[0])
    bb_p, tb_p = _tile_shape(*x_prompt.shape[:2])
    bb_s, tb_s = _tile_shape(*x_sample.shape[:2])
    y_p, a_p, q_p, s_p = _layer_call(x_prompt, None, params, bb=bb_p, tb=tb_p,
                                     c=math.gcd(x_prompt.shape[1], CHUNK), name="hybrid_prompt")
    y_s, a_s, q_s, s_s = _layer_call(x_sample, (cache_conv_a, cache_conv_qkv, state_gdn), params,
                                     bb=bb_s, tb=tb_s, c=math.gcd(x_sample.shape[1], CHUNK),
                                     name="hybrid_sample")
    return (y_p, y_s, a_p, q_p, s_p, a_s, q_s, s_s)
```

```python
import functools
import math

import jax
import jax.numpy as jnp
from jax import lax
from jax.experimental import pallas as pl
from jax.experimental.pallas import tpu as pltpu

D_MODEL = 1024
D_CONV = 512
CONV_A_WIDTH = 3
N_HEADS = 4
HEAD_DIM = 128
D_GDN = N_HEADS * HEAD_DIM
D_QKV = 3 * D_GDN
CONV_QKV_WIDTH = 4
D_MIX = D_CONV + D_GDN
D_A = 4 * D_CONV
D_MAIN = D_A + D_QKV + D_GDN
CHUNK = 64
RMS_EPS = 1e-6
L2_EPS = 1e-6

LANES = 128
SUBLANES = 8
PAD = SUBLANES
GATE_W = 2 * LANES
PROJ_NBLK = 256
TILE_ROWS = 256
VMEM_LIMIT_BYTES = 52 * 1024 * 1024

EARLY_PIECES = (16, 2, 3, 4, 5, 0, 6, 1, 7)
EARLY_FOLLOWERS = (("gates",), ("conv_in",), ("a", 0), ("a", 1), ("a", 2), ("a", 3))
_QKV_PIECE = lambda m: (D_A // PROJ_NBLK + m, (("b", 2 * m), ("b", 2 * m + 1)))
LATE_FIRST = (_QKV_PIECE(0), _QKV_PIECE(2))
LATE_REST = (_QKV_PIECE(1), _QKV_PIECE(3), _QKV_PIECE(4), _QKV_PIECE(5),
             (14, ()), (15, (("silu_zg",),)))
HEAD_GROUPS = ((0, (0,)), (1, (1,)), (2, (2,)), (3, (3,)))
OUT_PROJ_ROUNDS = (0, 0, 3, 3)

F32 = jnp.float32
BF16 = jnp.bfloat16


def _dot(a, b):
    return jnp.dot(a, b, preferred_element_type=F32)


def _dot_nt(a, b):
    return lax.dot_general(a, b, (((1,), (1,)), ((), ())), preferred_element_type=F32)


def _dot_tn(a, b):
    return lax.dot_general(a, b, (((0,), (0,)), ((), ())), preferred_element_type=F32)


def _silu_of_half(h):
    return h + h * jnp.tanh(h)


def _silu(z):
    return _silu_of_half(0.5 * z)


def _split3(x):
    hi = x.astype(BF16)
    r1 = x - hi.astype(F32)
    mid = r1.astype(BF16)
    lo = (r1 - mid.astype(F32)).astype(BF16)
    return jnp.concatenate([hi, mid, lo], axis=0)


def _round_robin(generators):
    alive = list(generators)
    while alive:
        still = []
        for g in alive:
            try:
                next(g)
                still.append(g)
            except StopIteration:
                pass
        alive = still
        yield


def _interleave(generators):
    for _ in _round_robin(generators):
        pass


def _chunk_head_precompute(out, key, c, qh, kh, load_v, bcol, ecol, rcol, diff, incl, strict, eye):
    decay = jnp.exp(diff + incl)
    kb = kh.astype(BF16)
    scores = _dot_nt(jnp.concatenate([kb, qh.astype(BF16)], axis=0), kb)
    qg = (qh * ecol).astype(BF16)
    k_rem = (kh * rcol).astype(BF16)
    k_rhs = kh * (bcol * ecol)
    yield
    a_mat = scores[:c] * (decay * strict) * bcol
    attn = (scores[c:] * decay).astype(BF16)
    levels = int(math.log2(c)) - 1
    ab = a_mat.astype(BF16)
    x = _dot(ab, ab)
    yield
    t = eye - a_mat
    for k in range(1, levels + 1):
        xb = x.astype(BF16)
        if k < levels:
            tx = _dot(jnp.concatenate([t.astype(BF16), xb], axis=0), xb)
            yield
            t = t + tx[:c]
            x = tx[c:]
        else:
            tx = _dot(t.astype(BF16), xb)
            yield
            t = t + tx
    rhs = jnp.concatenate([load_v() * bcol, k_rhs], axis=1)
    uw = _dot(t.astype(BF16), rhs.astype(BF16))
    yield
    out[key] = dict(u=uw[:, :HEAD_DIM], w=uw[:, HEAD_DIM:].astype(BF16), qg=qg, attn=attn,
                    k_rem=k_rem)


def _chunk_head_recurrence(pre, state, key, decay_last, finish):
    s_old = state[key]
    c = pre["u"].shape[0]
    wq_s = _dot(jnp.concatenate([pre["w"], pre["qg"]], axis=0), s_old.astype(BF16))
    yield
    v_new = (pre["u"] - wq_s[:c]).astype(BF16)
    o = wq_s[c:] + _dot(pre["attn"], v_new)
    state[key] = s_old * decay_last + _dot_tn(pre["k_rem"], v_new)
    yield
    finish(o)


def _normalised_input(x_ref, lnpre_ref):
    bb, tb, _ = x_ref.shape
    x = x_ref[...].reshape(bb * tb, D_MODEL)
    var = jnp.mean(x * x, axis=-1, keepdims=True)
    return (x * lax.rsqrt(var + RMS_EPS) * lnpre_ref[...]).astype(BF16)


def _staged_rows(b, tb, back=0):
    first = b * (PAD + tb) + PAD - back
    return slice(first, first + tb)


def _project_piece(bb, tb, j, hb, win_ref, wgate_ref, proj_s, qc_s, zraw_s, gate_s):
    lo = j * PROJ_NBLK
    hi = (j + 1) * PROJ_NBLK
    res = _dot(hb, win_ref[:, lo:hi] if lo < D_MAIN else wgate_ref[...])
    if lo < D_A:
        proj_s[:, lo:hi] = res
    elif lo < D_A + D_QKV:
        for b in range(bb):
            qc_s[_staged_rows(b, tb), lo - D_A:hi - D_A] = res[b * tb:(b + 1) * tb]
    elif lo < D_MAIN:
        zraw_s[:, lo - D_A - D_QKV:hi - D_A - D_QKV] = res
    else:
        gate_s[...] = res


def _elementwise_pieces(bb, tb, cwa_ref, cwq_ref, alog_ref, dtb_ref, proj_s, ua_s, qc_s, zraw_s,
                        gate_s, qkv_s, zg_s, ya_s, g_s, beta_s):
    def causal_conv(buf_s, w_ref, width, cs, halved=False):
        taps = [w_ref[k:k + 1, cs] for k in range(width)]
        if halved:
            taps = [0.5 * w for w in taps]
        rows = []
        for b in range(bb):
            conv = taps[0] * buf_s[_staged_rows(b, tb, width - 1), cs]
            for k in range(1, width):
                conv = conv + taps[k] * buf_s[_staged_rows(b, tb, width - 1 - k), cs]
            rows.append(conv)
        return rows[0] if bb == 1 else jnp.concatenate(rows, axis=0)

    def conv_in():
        for j in range(D_CONV // LANES):
            cs = slice(j * LANES, (j + 1) * LANES)
            u = (proj_s[:, D_CONV + j * LANES:D_CONV + (j + 1) * LANES]
                 * proj_s[:, 2 * D_CONV + j * LANES:2 * D_CONV + (j + 1) * LANES])
            for b in range(bb):
                ua_s[_staged_rows(b, tb), cs] = u[b * tb:(b + 1) * tb]

    def group_a(j):
        cs = slice(j * LANES, (j + 1) * LANES)
        conv = causal_conv(ua_s, cwa_ref, CONV_A_WIDTH, cs)
        z = proj_s[:, 3 * D_CONV + j * LANES:3 * D_CONV + (j + 1) * LANES]
        ya_s[:, cs] = (proj_s[:, cs] * conv * _silu(z)).astype(BF16)

    def group_b(j):
        cs = slice(j * LANES, (j + 1) * LANES)
        a = _silu_of_half(causal_conv(qc_s, cwq_ref, CONV_QKV_WIDTH, cs, halved=True))
        if j < 2 * N_HEADS:
            rn = lax.rsqrt(jnp.sum(a * a, axis=-1, keepdims=True) + L2_EPS)
            if j < N_HEADS:
                rn = rn * (HEAD_DIM ** -0.5)
            a = a * rn
        qkv_s[:, cs] = a

    def gates():
        xs = gate_s[:, 0:LANES] + dtb_ref[...]
        softplus = jnp.maximum(xs, 0.0) + jnp.log1p(jnp.exp(-jnp.abs(xs)))
        g_s[...] = -jnp.exp(alog_ref[...]) * softplus
        beta_s[...] = 1.0 / (1.0 + jnp.exp(-gate_s[:, LANES:2 * LANES]))

    def silu_zg():
        zg_s[...] = _silu(zraw_s[...])

    return {"gates": gates, "conv_in": conv_in, "a": group_a, "b": group_b, "silu_zg": silu_zg}


def _layer_kernel(*refs, bb, tb, c, n_t, n_tiles, has_init):
    refs = list(refs)
    xprev_ref, xnext_ref = refs.pop(0), refs.pop(0)
    if has_init:
        bufa_ref, bufq_ref, s0_ref = refs.pop(0), refs.pop(0), refs.pop(0)
    (lnpre_ref, win_ref, wgate_ref, cwa_ref, cwq_ref, alog_ref, dtb_ref, gnw_ref, wout_ref, lnpost_ref,
     y_ref, na_ref, nq_ref, ns_ref,
     h_s, proj_s, ua_s, qc_s, zraw_s, gate_s, qkv_s, zg_s, g_s, beta_s, ya_s, mix_s, out_s) = refs
    nc = tb // c
    s = pl.program_id(0)
    proj_bufs = (win_ref, wgate_ref, proj_s, qc_s, zraw_s, gate_s)
    pieces = _elementwise_pieces(bb, tb, cwa_ref, cwq_ref, alog_ref, dtb_ref, proj_s, ua_s, qc_s,
                                 zraw_s, gate_s, qkv_s, zg_s, ya_s, g_s, beta_s)
    tail_rows = lambda b, width: slice(b * (PAD + tb) + PAD - (width - 1), b * (PAD + tb) + PAD)
    last_rows = lambda b, width: slice((b + 1) * (PAD + tb) - (width - 1), (b + 1) * (PAD + tb))

    @pl.when(s == 0)
    def _first_tile_early_pieces():
        hb = _normalised_input(xprev_ref, lnpre_ref)
        h_s[...] = hb
        for j in EARLY_PIECES:
            _project_piece(bb, tb, j, hb, *proj_bufs)
        mix_s[...] = jnp.zeros(mix_s.shape, BF16)

    @pl.when((s % n_t == 0) & (s < n_tiles))
    def _load_stream_state():
        for b in range(bb):
            if has_init:
                ua_s[tail_rows(b, CONV_A_WIDTH), :] = bufa_ref[0, b]
                qc_s[tail_rows(b, CONV_QKV_WIDTH), :] = bufq_ref[0, b]
                ns_ref[0, b] = s0_ref[0, b]
            else:
                ua_s[tail_rows(b, CONV_A_WIDTH), :] = jnp.zeros((CONV_A_WIDTH - 1, D_CONV), F32)
                qc_s[tail_rows(b, CONV_QKV_WIDTH), :] = jnp.zeros((CONV_QKV_WIDTH - 1, D_QKV), F32)
                ns_ref[0, b] = jnp.zeros((N_HEADS, HEAD_DIM, HEAD_DIM), F32)

    hb = h_s[...]
    for kind, *args in EARLY_FOLLOWERS:
        pieces[kind](*args)

    def project_pieces(schedule):
        for j, followers in schedule:
            _project_piece(bb, tb, j, hb, *proj_bufs)
            for kind, *args in followers:
                pieces[kind](*args)
            yield

    _interleave([project_pieces(LATE_FIRST)])

    row = lax.broadcasted_iota(jnp.int32, (c, c), 0)
    col = lax.broadcasted_iota(jnp.int32, (c, c), 1)
    incl = jnp.where(row >= col, 0.0, -jnp.inf)
    strict = jnp.where(row > col, 1.0, 0.0)
    eye = jnp.where(row == col, 1.0, 0.0)
    row3 = lax.broadcasted_iota(jnp.int32, (c, 3 * c), 0)
    col3 = lax.broadcasted_iota(jnp.int32, (c, 3 * c), 1) & (c - 1)
    cumsum_mat = jnp.where(row3 >= col3, 1.0, 0.0).astype(BF16)

    pre = {}
    decay_last = {}
    chunks = [(b, n, slice(b * tb + n * c, b * tb + (n + 1) * c))
              for b in range(bb) for n in range(nc)]
    gate_terms = {}
    for b, n, rs in chunks:
        gb = _dot(cumsum_mat, _split3(g_s[rs, :]))
        gl = gb[c - 1:c, :]
        gb_rows = jnp.concatenate([gb, jnp.zeros((LANES - c, LANES), F32)], axis=0).T
        gate_terms[b, n] = (gb, gb_rows, jnp.exp(gb), jnp.exp(gl - gb), jnp.exp(gl), beta_s[rs, :])

    def head_group_chains(start_round, heads):
        for _ in range(start_round):
            yield
        chains = []
        for b, n, rs in chunks:
            gb, gb_rows, e_gb, e_rem, e_gl, beta_c = gate_terms[b, n]
            for hd in heads:
                qh = qkv_s[rs, hd * HEAD_DIM:(hd + 1) * HEAD_DIM]
                kh = qkv_s[rs, D_GDN + hd * HEAD_DIM:D_GDN + (hd + 1) * HEAD_DIM]
                load_v = functools.partial(
                    lambda rows, h: qkv_s[rows, 2 * D_GDN + h * HEAD_DIM:2 * D_GDN + (h + 1) * HEAD_DIM],
                    rs, hd)
                diff = gb[:, hd:hd + 1] - gb_rows[hd:hd + 1, 0:c]
                decay_last[b, n, hd] = e_gl[:, hd:hd + 1]
                chains.append(_chunk_head_precompute(
                    pre, (b, n, hd), c, qh, kh, load_v, beta_c[:, hd:hd + 1], e_gb[:, hd:hd + 1],
                    e_rem[:, hd:hd + 1], diff, incl, strict, eye))
        yield from _round_robin(chains)

    def previous_tile_output():
        mix = mix_s[...]
        this_round = 0
        for j, at_round in enumerate(OUT_PROJ_ROUNDS):
            while this_round < at_round:
                yield
                this_round += 1
            cols = slice(j * PROJ_NBLK, (j + 1) * PROJ_NBLK)
            out_s[:, cols] = _dot(mix, wout_ref[:, cols])
        yield
        out = out_s[...]
        out_var = jnp.mean(out * out, axis=-1, keepdims=True)
        y = (xprev_ref[...].reshape(bb * tb, D_MODEL)
             + out * lax.rsqrt(out_var + RMS_EPS) * lnpost_ref[...])
        y_ref[...] = y.reshape(bb, tb, D_MODEL)

    _interleave([project_pieces(LATE_REST), previous_tile_output()]
                + [head_group_chains(start, heads) for start, heads in HEAD_GROUPS])

    def finish_head(rs, hd):
        def finish(o):
            o_var = jnp.mean(o * o, axis=-1, keepdims=True)
            o = o * lax.rsqrt(o_var + RMS_EPS) * gnw_ref[...]
            mix_s[rs, D_CONV + hd * HEAD_DIM:D_CONV + (hd + 1) * HEAD_DIM] = (
                o * zg_s[rs, hd * HEAD_DIM:(hd + 1) * HEAD_DIM]).astype(BF16)
        return finish

    state = {(b, hd): ns_ref[0, b, hd] for b in range(bb) for hd in range(N_HEADS)}
    mix_s[:, 0:D_CONV] = ya_s[...]

    def recurrence():
        for n in range(nc):
            yield from _round_robin([
                _chunk_head_recurrence(pre[b, n, hd], state, (b, hd), decay_last[b, n, hd],
                                       finish_head(slice(b * tb + n * c, b * tb + (n + 1) * c), hd))
                for b in range(bb) for hd in range(N_HEADS)])

    def next_tile_early_pieces():
        hb_next = _normalised_input(xnext_ref, lnpre_ref)
        h_s[...] = hb_next
        for j in EARLY_PIECES:
            _project_piece(bb, tb, j, hb_next, *proj_bufs)
            yield

    _interleave([recurrence(), next_tile_early_pieces()])

    @pl.when(s < n_tiles)
    def _carry_stream_state():
        for (b, hd), value in state.items():
            ns_ref[0, b, hd] = value
        for b in range(bb):
            tail_a = ua_s[last_rows(b, CONV_A_WIDTH), :]
            ua_s[tail_rows(b, CONV_A_WIDTH), :] = tail_a
            na_ref[0, b] = tail_a
            tail_q = qc_s[last_rows(b, CONV_QKV_WIDTH), :]
            qc_s[tail_rows(b, CONV_QKV_WIDTH), :] = tail_q
            nq_ref[0, b] = tail_q


def _layer_call(x, init, params, *, bb, tb, c, name):
    bsz, seq, _ = x.shape
    assert bsz % bb == 0 and seq % tb == 0 and tb % c == 0 and tb % SUBLANES == 0 and c <= LANES
    has_init = init is not None
    n_t = seq // tb
    n_tiles = (bsz // bb) * n_t
    rows = bb * tb

    def this_tile(s):
        return jnp.minimum(s, n_tiles - 1)

    def prev_tile(s):
        return jnp.maximum(s - 1, 0)

    def next_tile(s):
        return jnp.minimum(s + 1, n_tiles - 1)

    tile_spec = lambda tile: pl.BlockSpec((bb, tb, D_MODEL), lambda s: (tile(s) // n_t, tile(s) % n_t, 0))
    stream_specs = [
        pl.BlockSpec((1, bb, CONV_A_WIDTH - 1, D_CONV), lambda s: (0, this_tile(s) // n_t, 0, 0)),
        pl.BlockSpec((1, bb, CONV_QKV_WIDTH - 1, D_QKV), lambda s: (0, this_tile(s) // n_t, 0, 0)),
        pl.BlockSpec((1, bb, N_HEADS, HEAD_DIM, HEAD_DIM),
                     lambda s: (0, this_tile(s) // n_t, 0, 0, 0)),
    ]
    in_specs = [tile_spec(prev_tile), tile_spec(next_tile)]
    operands = [x, x]
    if has_init:
        in_specs += stream_specs
        operands += list(init)
    in_specs += [pl.BlockSpec(p.shape, lambda s, nd=p.ndim: (0,) * nd) for p in params]
    operands += list(params)
    out_shape = (
        jax.ShapeDtypeStruct(x.shape, x.dtype),
        jax.ShapeDtypeStruct((1, bsz, CONV_A_WIDTH - 1, D_CONV), F32),
        jax.ShapeDtypeStruct((1, bsz, CONV_QKV_WIDTH - 1, D_QKV), F32),
        jax.ShapeDtypeStruct((1, bsz, N_HEADS, HEAD_DIM, HEAD_DIM), F32),
    )
    out_specs = [tile_spec(prev_tile)] + stream_specs
    scratch = [
        pltpu.VMEM((rows, D_MODEL), BF16),
        pltpu.VMEM((rows, D_A), F32),
        pltpu.VMEM((bb * (PAD + tb), D_CONV), F32),
        pltpu.VMEM((bb * (PAD + tb), D_QKV), F32),
        pltpu.VMEM((rows, D_GDN), F32),
        pltpu.VMEM((rows, GATE_W), F32),
        pltpu.VMEM((rows, D_QKV), F32),
        pltpu.VMEM((rows, D_GDN), F32),
        pltpu.VMEM((rows, LANES), F32),
        pltpu.VMEM((rows, LANES), F32),
        pltpu.VMEM((rows, D_CONV), BF16),
        pltpu.VMEM((rows, D_MIX), BF16),
        pltpu.VMEM((rows, D_MODEL), F32),
    ]
    return pl.pallas_call(
        functools.partial(_layer_kernel, bb=bb, tb=tb, c=c, n_t=n_t, n_tiles=n_tiles,
                          has_init=has_init),
        grid=(n_tiles + 1,),
        in_specs=in_specs,
        out_specs=out_specs,
        out_shape=out_shape,
        scratch_shapes=scratch,
        compiler_params=pltpu.CompilerParams(
            dimension_semantics=("arbitrary",),
            vmem_limit_bytes=VMEM_LIMIT_BYTES),
        name=name,
    )(*operands)


def _tile_shape(bsz, seq):
    tb = min(seq, TILE_ROWS)
    bb = max(1, math.gcd(bsz, TILE_ROWS // tb))
    return bb, tb


def _prepare_params(ln_pre, w_in, conv_a_w, conv_qkv_w, a_log, dt_bias, gdn_norm_w, w_out, ln_post):
    w = w_in.astype(BF16)
    gap = jnp.zeros((D_MODEL, LANES - N_HEADS), BF16)
    w_gate = jnp.concatenate([w[:, D_MAIN:D_MAIN + N_HEADS], gap, w[:, D_MAIN + N_HEADS:], gap], axis=1)
    pad_heads = lambda v: jnp.pad(v, (0, LANES - N_HEADS)).reshape(1, LANES)
    return (ln_pre.reshape(1, D_MODEL), w, w_gate, conv_a_w, conv_qkv_w,
            pad_heads(a_log), pad_heads(dt_bias), gdn_norm_w.reshape(1, HEAD_DIM),
            w_out.astype(BF16), ln_post.reshape(1, D_MODEL))


def kernel(x_prompt, x_sample, cache_conv_a, cache_conv_qkv, state_gdn, ln_pre, w_in, conv_a_w,
           conv_qkv_w, a_log, dt_bias, gdn_norm_w, w_out, ln_post):
    assert ln_pre.shape[0] == 1, "single-layer kernel"
    params = _prepare_params(ln_pre[0], w_in[0], conv_a_w[0], conv_qkv_w[0], a_log[0], dt_bias[0],
                             gdn_norm_w[0], w_out[0], ln_post[0])
    bb_p, tb_p = _tile_shape(*x_prompt.shape[:2])
    bb_s, tb_s = _tile_shape(*x_sample.shape[:2])
    y_p, a_p, q_p, s_p = _layer_call(x_prompt, None, params, bb=bb_p, tb=tb_p,
                                     c=math.gcd(x_prompt.shape[1], CHUNK), name="hybrid_prompt")
    y_s, a_s, q_s, s_s = _layer_call(x_sample, (cache_conv_a, cache_conv_qkv, state_gdn), params,
                                     bb=bb_s, tb=tb_s, c=math.gcd(x_sample.shape[1], CHUNK),
                                     name="hybrid_sample")
    return (y_p, y_s, a_p, q_p, s_p, a_s, q_s, s_s)
```
